```python
import math
import jax, jax.numpy as jnp
from jax import lax
import numpy as np

D_MODEL = 1024
BATCH = 4
SEQ = 8192
DEPTH = 4

GRID_W = 64
CTX_LEN = 256
Q_BLOCK = 128
ROPE_THETA = 10000.0
NORM_EPS = 1e-6
MLA_HEADS = 8
MLA_Q_LORA = 384
MLA_KV_LORA = 256
MLA_NOPE = 64
MLA_ROPE = 32
MLA_V = 64
MLA_SCALE = (MLA_NOPE + MLA_ROPE) ** -0.5
GQA_HEADS = 8
GQA_KV_HEADS = 2
GQA_GROUP = GQA_HEADS // GQA_KV_HEADS
GQA_HEAD_DIM = 64
GQA_SCALE = GQA_HEAD_DIM ** -0.5
FFN_DENSE = 2816
N_EXPERTS = 8
TOP_K = 2
FFN_EXPERT = 3584
N_DENSE = (DEPTH + 1) // 2
N_MOE = DEPTH // 2
IN_SIZES = (MLA_Q_LORA, MLA_KV_LORA, MLA_ROPE,
            GQA_HEADS * GQA_HEAD_DIM, GQA_KV_HEADS * GQA_HEAD_DIM, GQA_KV_HEADS * GQA_HEAD_DIM,
            D_MODEL, D_MODEL)
D_IN = sum(IN_SIZES)

kernel_name = 'hybrid_mla_gqa_moe_prefix_ctx_trunk'


def rms_norm(x, g):
    xf = x.astype(jnp.float32)
    y = xf * lax.rsqrt(jnp.mean(xf * xf, axis=-1, keepdims=True) + NORM_EPS)
    return (y * g.astype(jnp.float32)).astype(x.dtype)


def modulate(h, shift, scale):
    return h * (1 + scale) + shift


def split_cols(y, sizes):
    offs, s = [], 0
    for n in sizes[:-1]:
        s += n
        offs.append(s)
    return jnp.split(y, offs, axis=-1)


def axial_rope_tables(n_ctx, n_lat, dim):
    half = dim // 2
    inv = ROPE_THETA ** (-(jnp.arange(half // 2, dtype=jnp.float32) * 2.0) / half)
    rows = n_lat // GRID_W
    t = jnp.arange(rows * GRID_W)
    row = (t // GRID_W).astype(jnp.float32)
    col = (t % GRID_W).astype(jnp.float32)
    zeros = jnp.zeros((n_ctx, half // 2), jnp.float32)
    ang_r = jnp.concatenate([zeros, row[:, None] * inv], axis=0)[:, None, :]
    ang_c = jnp.concatenate([zeros, col[:, None] * inv], axis=0)[:, None, :]
    return (jnp.cos(ang_r), jnp.sin(ang_r), jnp.cos(ang_c), jnp.sin(ang_c))


def _rotate(xa, cos, sin):
    x1, x2 = jnp.split(xa, 2, axis=-1)
    cos = cos.astype(xa.dtype)
    sin = sin.astype(xa.dtype)
    return jnp.concatenate([x1 * cos - x2 * sin, x1 * sin + x2 * cos], axis=-1)


def rope_2d(x, tabs):
    cos_r, sin_r, cos_c, sin_c = tabs
    half = x.shape[-1] // 2
    return jnp.concatenate([_rotate(x[..., :half], cos_r, sin_r),
                            _rotate(x[..., half:], cos_c, sin_c)], axis=-1)


def attn_blocks(q, k, v, scale):
    b, s, kh, g, dk = q.shape
    nb = s // Q_BLOCK
    qb = jnp.moveaxis(q.reshape(b, nb, Q_BLOCK, kh, g, dk), 1, 0)

    def one_block(q_blk):
        logits = jnp.einsum('bqhgd,bkhd->bhgqk', q_blk, k,
                            preferred_element_type=jnp.float32) * scale
        p = jax.nn.softmax(logits, axis=-1).astype(v.dtype)
        return jnp.einsum('bhgqk,bkhe->bqhge', p, v)

    out = lax.map(one_block, qb)
    return jnp.moveaxis(out, 0, 1).reshape(b, s, kh * g, v.shape[-1])


def attend(q, k, v, n_ctx, latent_only, scale):
    lat = attn_blocks(q[:, n_ctx:], k, v, scale)
    if latent_only:
        return lat
    ctx_out = attn_blocks(q[:, :n_ctx], k[:, :n_ctx], v[:, :n_ctx], scale)
    return jnp.concatenate([ctx_out, lat], axis=1)


def token_mixer(h, n_ctx, latent_only, w_in, q_norm_g, w_qb, kv_norm_g, w_kvb,
                gqa_q_g, gqa_k_g, w_o_mla, w_o_gqa, w_out, tabs_mla, tabs_gqa):
    b, t, _ = h.shape
    q_c, kv_c, k_pe, q_g, k_g, v_g, gate_mla, gate_gqa = split_cols(h @ w_in, IN_SIZES)

    q = (rms_norm(q_c, q_norm_g) @ w_qb).reshape(b, t, MLA_HEADS, MLA_NOPE + MLA_ROPE)
    q = jnp.concatenate([q[..., :MLA_NOPE], rope_2d(q[..., MLA_NOPE:], tabs_mla)], axis=-1)
    kv = (rms_norm(kv_c, kv_norm_g) @ w_kvb).reshape(b, t, MLA_HEADS, MLA_NOPE + MLA_V)
    k_pe = rope_2d(k_pe[:, :, None, :], tabs_mla)
    k = jnp.concatenate([kv[..., :MLA_NOPE],
                         jnp.broadcast_to(k_pe, (b, t, MLA_HEADS, MLA_ROPE))], axis=-1)
    v = kv[..., MLA_NOPE:]

    qg = rope_2d(rms_norm(q_g.reshape(b, t, GQA_HEADS, GQA_HEAD_DIM), gqa_q_g), tabs_gqa)
    qg = qg.reshape(b, t, GQA_KV_HEADS, GQA_GROUP, GQA_HEAD_DIM)
    kg = rope_2d(rms_norm(k_g.reshape(b, t, GQA_KV_HEADS, GQA_HEAD_DIM), gqa_k_g), tabs_gqa)
    vg = v_g.reshape(b, t, GQA_KV_HEADS, GQA_HEAD_DIM)

    start = n_ctx if latent_only else 0
    t_out = t - start
    o_mla = attend(q[:, :, :, None, :], k, v, n_ctx, latent_only, MLA_SCALE)
    o_mla = o_mla.reshape(b, t_out, MLA_HEADS * MLA_V) @ w_o_mla
    o_gqa = attend(qg, kg, vg, n_ctx, latent_only, GQA_SCALE)
    o_gqa = o_gqa.reshape(b, t_out, GQA_HEADS * GQA_HEAD_DIM) @ w_o_gqa

    merged = (jax.nn.sigmoid(gate_mla[:, start:]) * o_mla
              + jax.nn.sigmoid(gate_gqa[:, start:]) * o_gqa)
    return merged @ w_out


def swiglu(h, w1, w3, w2):
    return (jax.nn.silu(h @ w1) * (h @ w3)) @ w2


def moe_swiglu(h, router, w1, w3, w2):
    logits = (h @ router).astype(jnp.float32)
    top_val, top_idx = lax.top_k(logits, TOP_K)
    top_w = jax.nn.softmax(top_val, axis=-1)
    combine = jnp.sum(jax.nn.one_hot(top_idx, N_EXPERTS, dtype=jnp.float32) * top_w[..., None],
                      axis=-2).astype(h.dtype)
    y = jnp.zeros_like(h)
    for e in range(N_EXPERTS):
        y = y + combine[..., e:e + 1] * swiglu(h, w1[e], w3[e], w2[e])
    return y


def setup_inputs(seed: int = 0) -> dict:
    key = jax.random.key(seed)
    k = jax.random.split(key, 32)
    f32 = jnp.float32

    def w(i, shape, fan_in, s=1.0):
        return jax.random.normal(k[i], shape, f32) * (s * fan_in ** -0.5)

    def gain(i, shape):
        return 1.0 + 0.1 * jax.random.normal(k[i], shape, f32)

    D = D_MODEL
    return {
        'x': jax.random.normal(k[0], (BATCH, SEQ, D), f32),
        'c': jax.random.normal(k[1], (BATCH, D), f32),
        'ctx': jax.random.normal(k[2], (BATCH, CTX_LEN, D), f32),
        'c_ctx': jax.random.normal(k[3], (D,), f32),
        'w_ada': w(4, (DEPTH, D, 6 * D), D, 0.5),
        'b_ada': 0.1 * jax.random.normal(k[5], (DEPTH, 6 * D), f32),
        'norm1_g': gain(6, (DEPTH, D)),
        'norm2_g': gain(7, (DEPTH, D)),
        'w_in': w(8, (DEPTH, D, D_IN), D),
        'mla_q_norm_g': gain(9, (DEPTH, MLA_Q_LORA)),
        'mla_w_qb': w(10, (DEPTH, MLA_Q_LORA, MLA_HEADS * (MLA_NOPE + MLA_ROPE)), MLA_Q_LORA),
        'mla_kv_norm_g': gain(11, (DEPTH, MLA_KV_LORA)),
        'mla_w_kvb': w(12, (DEPTH, MLA_KV_LORA, MLA_HEADS * (MLA_NOPE + MLA_V)), MLA_KV_LORA),
        'gqa_q_norm_g': gain(13, (DEPTH, GQA_HEAD_DIM)),
        'gqa_k_norm_g': gain(14, (DEPTH, GQA_HEAD_DIM)),
        'w_o_mla': w(15, (DEPTH, MLA_HEADS * MLA_V, D), MLA_HEADS * MLA_V),
        'w_o_gqa': w(16, (DEPTH, GQA_HEADS * GQA_HEAD_DIM, D), GQA_HEADS * GQA_HEAD_DIM),
        'w_out': w(17, (DEPTH, D, D), D),
        'dense_w1': w(18, (N_DENSE, D, FFN_DENSE), D),
        'dense_w3': w(19, (N_DENSE, D, FFN_DENSE), D),
        'dense_w2': w(20, (N_DENSE, FFN_DENSE, D), FFN_DENSE),
        'moe_router': w(21, (N_MOE, D, N_EXPERTS), D),
        'moe_w1': w(22, (N_MOE, N_EXPERTS, D, FFN_EXPERT), D),
        'moe_w3': w(23, (N_MOE, N_EXPERTS, D, FFN_EXPERT), D),
        'moe_w2': w(24, (N_MOE, N_EXPERTS, FFN_EXPERT, D), FFN_EXPERT),
        'final_norm_g': gain(25, (D,)),
    }


def reference(x, c, ctx, c_ctx, w_ada, b_ada, norm1_g, norm2_g, w_in,
              mla_q_norm_g, mla_w_qb, mla_kv_norm_g, mla_w_kvb,
              gqa_q_norm_g, gqa_k_norm_g, w_o_mla, w_o_gqa, w_out,
              dense_w1, dense_w3, dense_w2, moe_router, moe_w1, moe_w3, moe_w2,
              final_norm_g):
    b, n_lat, d = x.shape
    n_ctx = ctx.shape[1]
    tabs_mla = axial_rope_tables(n_ctx, n_lat, MLA_ROPE)
    tabs_gqa = axial_rope_tables(n_ctx, n_lat, GQA_HEAD_DIM)
    silu_c = jax.nn.silu(c)
    silu_cc = jax.nn.silu(c_ctx)

    for l in range(DEPTH):
        last = l == DEPTH - 1
        mod_lat = (silu_c @ w_ada[l] + b_ada[l]).reshape(b, 1, 6, d)
        mod_ctx = (silu_cc @ w_ada[l] + b_ada[l]).reshape(1, 1, 6, d)
        sh1_l, sc1_l, g1_l, sh2_l, sc2_l, g2_l = [mod_lat[:, :, i] for i in range(6)]
        sh1_c, sc1_c, g1_c, sh2_c, sc2_c, g2_c = [mod_ctx[:, :, i] for i in range(6)]

        h = jnp.concatenate([modulate(rms_norm(ctx, norm1_g[l]), sh1_c, sc1_c),
                             modulate(rms_norm(x, norm1_g[l]), sh1_l, sc1_l)], axis=1)
        mix = token_mixer(h, n_ctx, last, w_in[l], mla_q_norm_g[l], mla_w_qb[l],
                          mla_kv_norm_g[l], mla_w_kvb[l], gqa_q_norm_g[l], gqa_k_norm_g[l],
                          w_o_mla[l], w_o_gqa[l], w_out[l], tabs_mla, tabs_gqa)
        x = x + g1_l * mix[:, -n_lat:]
        if not last:
            ctx = ctx + g1_c * mix[:, :n_ctx]

        h2_lat = modulate(rms_norm(x, norm2_g[l]), sh2_l, sc2_l)
        if last:
            h2 = h2_lat
        else:
            h2 = jnp.concatenate([modulate(rms_norm(ctx, norm2_g[l]), sh2_c, sc2_c), h2_lat], axis=1)
        if l % 2 == 0:
            f = swiglu(h2, dense_w1[l // 2], dense_w3[l // 2], dense_w2[l // 2])
        else:
            f = moe_swiglu(h2, moe_router[l // 2], moe_w1[l // 2], moe_w3[l // 2], moe_w2[l // 2])
        x = x + g2_l * f[:, -n_lat:]
        if not last:
            ctx = ctx + g2_c * f[:, :n_ctx]

    return rms_norm(x, final_norm_g)
```

```python
import functools
import math

import numpy as np
import jax
import jax.numpy as jnp
from jax import lax
from jax.experimental import pallas as pl
from jax.experimental.pallas import tpu as pltpu

F32 = jnp.float32
BF16 = jnp.bfloat16

GRID_W = 64
ROPE_THETA = 10000.0
NORM_EPS = 1e-6
MLA_HEADS = 8
MLA_Q_LORA = 384
MLA_KV_LORA = 256
MLA_NOPE = 64
MLA_ROPE = 32
MLA_V = 64
MLA_SCALE = (MLA_NOPE + MLA_ROPE) ** -0.5
GQA_HEADS = 8
GQA_KV_HEADS = 2
GQA_GROUP = GQA_HEADS // GQA_KV_HEADS
GQA_HEAD_DIM = 64
GQA_SCALE = GQA_HEAD_DIM ** -0.5
N_EXPERTS = 8
LOG2E = math.log2(math.e)

LANE = 128
VMEM_LIMIT = 56 * 1024 * 1024

C_QC = 0
C_KVC = C_QC + MLA_Q_LORA
C_KPE = C_KVC + MLA_KV_LORA
C_KPES = C_KPE + LANE
C_QG = C_KPES + LANE
C_QGS = C_QG + 512
C_KG = C_QGS + 512
C_KGS = C_KG + 256
C_VG = C_KGS + 256
C_GATE = C_VG + 256
C_END = C_GATE + 2048

T_AM, T_BM, T_CK, T_SK, T_AQ, T_BQ, T_AK, T_BK = range(8)


def _cparams(sem):
    return pltpu.CompilerParams(dimension_semantics=sem, vmem_limit_bytes=VMEM_LIMIT)


def _const_spec(shape):
    nd = len(shape)
    return pl.BlockSpec(shape, lambda *_: (0,) * nd)


def _dot(a, b):
    return jnp.dot(a, b, preferred_element_type=F32)


def _rms(x):
    return x * lax.rsqrt(jnp.mean(x * x, axis=-1, keepdims=True) + NORM_EPS)


def _ada_kernel(cc_ref, w_ref, b_ref, o_ref):
    a = cc_ref[...]
    a = a * jax.nn.sigmoid(a)
    a_hi = a.astype(BF16)
    a_lo = (a - a_hi.astype(F32)).astype(BF16)
    w = w_ref[0]
    w_hi = w.astype(BF16)
    w_lo = (w - w_hi.astype(F32)).astype(BF16)
    acc = _dot(a_hi, w_hi) + _dot(a_lo, w_hi) + _dot(a_hi, w_lo)
    o_ref[0] = acc + b_ref[0]


def _ada_call(cc, w_ada, b_ada):
    depth, d, n = w_ada.shape
    rows = cc.shape[0]
    tn = 1536
    return pl.pallas_call(
        _ada_kernel,
        grid=(depth, n // tn),
        in_specs=[
            pl.BlockSpec((rows, d), lambda l, j: (0, 0)),
            pl.BlockSpec((1, d, tn), lambda l, j: (l, 0, j)),
            pl.BlockSpec((1, 1, tn), lambda l, j: (l, 0, j)),
        ],
        out_specs=pl.BlockSpec((1, rows, tn), lambda l, j: (l, 0, j)),
        out_shape=jax.ShapeDtypeStruct((depth, rows, n), F32),
        compiler_params=_cparams(("arbitrary", "arbitrary")),
        name="ada",
    )(cc, w_ada, b_ada.reshape(depth, 1, n))


def _group_mean_sq(x, bd):
    x2 = x * x
    hi = x2.astype(BF16)
    lo = (x2 - hi.astype(F32)).astype(BF16)
    return _dot(hi, bd) + _dot(lo, bd)


def _tile_lanes(t, n):
    return jnp.concatenate([t] * n, axis=1)


def _inproj_kernel(x_ref, mod_ref, g1_ref, win_ref, gq_ref, wqb_ref, wqbs_ref, gkv_ref,
                   wkk_ref, wkv_ref, bd_ref, tab_ref,
                   qm_ref, km_ref, vm_ref, qg_ref, kg_ref, vg_ref, gate_ref):
    x = x_ref[0]
    mod = mod_ref[0]
    h = (_rms(x) * g1_ref[...]) * (1.0 + mod[1:2]) + mod[0:1]
    hb = h.astype(BF16)

    def proj(lo, hi):
        return _dot(hb, win_ref[:, lo:hi])

    qcn = (_rms(proj(C_QC, C_KVC)) * gq_ref[...]).astype(BF16)
    q = _dot(qcn, wqb_ref[...])
    qs = _dot(qcn, wqbs_ref[...])
    am = _tile_lanes(tab_ref[T_AM], MLA_HEADS)
    bm = _tile_lanes(tab_ref[T_BM], MLA_HEADS)
    qm_ref[0] = (q * am + qs * bm).astype(BF16)

    kvn = (_rms(proj(C_KVC, C_KPE)) * gkv_ref[...]).astype(BF16)
    kpe = proj(C_KPE, C_KPES) * tab_ref[T_CK] + proj(C_KPES, C_QG) * tab_ref[T_SK]
    kn = _dot(kvn, wkk_ref[...])
    km_ref[0] = (kn + _tile_lanes(kpe, MLA_HEADS)).astype(BF16)
    vm_ref[0] = _dot(kvn, wkv_ref[...]).astype(BF16)

    bd = bd_ref[...]
    qg = proj(C_QG, C_QGS)
    qgs = proj(C_QGS, C_KG)
    rq = lax.rsqrt(_group_mean_sq(qg, bd) + NORM_EPS)
    aq = _tile_lanes(tab_ref[T_AQ], 4)
    bq = _tile_lanes(tab_ref[T_BQ], 4)
    qg_ref[0] = (rq * (qg * aq + qgs * bq)).astype(BF16)

    kg = proj(C_KG, C_KGS)
    kgs = proj(C_KGS, C_VG)
    rk = lax.rsqrt(_group_mean_sq(kg, bd[:256, :256]) + NORM_EPS)
    ak = _tile_lanes(tab_ref[T_AK], 2)
    bk = _tile_lanes(tab_ref[T_BK], 2)
    kg_ref[0] = (rk * (kg * ak + kgs * bk)).astype(BF16)
    vg_ref[0] = proj(C_VG, C_GATE).astype(BF16)

    gate_ref[0] = jax.nn.sigmoid(proj(C_GATE, C_END)).astype(BF16)


def _inproj_call(xs, mod, g1, lw, tabs, *, n_ctx, tm):
    b, t, d = xs.shape
    nct = n_ctx // tm
    ctx_row = b

    def tok(i, bb):
        return (bb, i, 0)

    def modmap(i, bb):
        return (jnp.where(i < nct, ctx_row, bb), 0, 0)

    out_w = (1024, 1024, 512, 512, 256, 256, 2048)
    return pl.pallas_call(
        _inproj_kernel,
        grid=(t // tm, b),
        in_specs=[
            pl.BlockSpec((1, tm, d), tok),
            pl.BlockSpec((1, 6, d), modmap),
            _const_spec((1, d)),
            _const_spec((d, C_END)),
            _const_spec((1, MLA_Q_LORA)),
            _const_spec((MLA_Q_LORA, 1024)),
            _const_spec((MLA_Q_LORA, 1024)),
            _const_spec((1, MLA_KV_LORA)),
            _const_spec((MLA_KV_LORA, 1024)),
            _const_spec((MLA_KV_LORA, 512)),
            _const_spec((512, 512)),
            pl.BlockSpec((8, tm, LANE), lambda i, bb: (0, i, 0)),
        ],
        out_specs=[pl.BlockSpec((1, tm, w), tok) for w in out_w],
        out_shape=[jax.ShapeDtypeStruct((b, t, w), BF16) for w in out_w],
        compiler_params=_cparams(("parallel", "arbitrary")),
        name="inproj",
    )(xs, mod, g1, lw["w_in"], lw["gq"], lw["wqb"], lw["wqbs"], lw["gkv"],
      lw["wkk"], lw["wkv"], lw["bd"], tabs)


def _attn_kernel(q_ref, k_ref, v_ref, o_ref, *, mla, tq, tk, n_ctx, t_all):
    qi = pl.program_id(2)
    nk = jnp.where(qi < n_ctx // tq, n_ctx // tk, t_all // tk)
    lane = lax.broadcasted_iota(jnp.int32, (tq, LANE), 1)
    lo_half = lane < 64
    if mla:
        qs = [q_ref[0, :, 0:LANE], q_ref[0, :, LANE:2 * LANE]]
        kcols = [(0, LANE), (LANE, 2 * LANE)]
    else:
        qs = []
        for p in range(2):
            qp = q_ref[0, :, p * LANE:(p + 1) * LANE]
            zero = jnp.zeros_like(qp)
            qs.append(jnp.where(lo_half, qp, zero))
            qs.append(jnp.where(lo_half, zero, qp))
        kcols = [(0, LANE)] * 4
    nh = len(qs)

    def body(c, carry):
        start = pl.multiple_of(c * tk, tk)
        v = v_ref[0, pl.ds(start, tk), :]
        out = []
        for j in range(nh):
            m, l, acc = carry[j]
            k = k_ref[0, pl.ds(start, tk), kcols[j][0]:kcols[j][1]]
            s = lax.dot_general(qs[j], k, (((1,), (1,)), ((), ())),
                                preferred_element_type=F32)
            m_new = jnp.maximum(m, jnp.max(s, axis=-1, keepdims=True))
            alpha = jnp.exp2(m - m_new)
            p = jnp.exp2(s - m_new)
            l_new = alpha * l + jnp.sum(p, axis=-1, keepdims=True)
            acc_new = alpha * acc + _dot(p.astype(BF16), v)
            out.append((m_new, l_new, acc_new))
        return tuple(out)

    init = tuple((jnp.full((tq, 1), -jnp.inf, F32), jnp.zeros((tq, 1), F32),
                  jnp.zeros((tq, LANE), F32)) for _ in range(nh))
    res = lax.fori_loop(0, nk, body, init)
    outs = [acc / l for (_, l, acc) in res]
    pairs = [jnp.where(lo_half, outs[2 * p], outs[2 * p + 1]) for p in range(nh // 2)]
    o_ref[0] = jnp.concatenate(pairs, axis=1).astype(BF16)


def _attn_call(q, k, v, *, mla, n_ctx, tq, tk):
    b, t, _ = q.shape
    if mla:
        groups, wq, wk, wo = MLA_HEADS // 2, 2 * LANE, 2 * LANE, LANE
    else:
        groups, wq, wk, wo = GQA_KV_HEADS, 2 * LANE, LANE, 2 * LANE
    kern = functools.partial(_attn_kernel, mla=mla, tq=tq, tk=tk, n_ctx=n_ctx, t_all=t)
    return pl.pallas_call(
        kern,
        grid=(b, groups, t // tq),
        in_specs=[
            pl.BlockSpec((1, tq, wq), lambda bb, g, i: (bb, i, g)),
            pl.BlockSpec((1, t, wk), lambda bb, g, i: (bb, 0, g)),
            pl.BlockSpec((1, t, LANE), lambda bb, g, i: (bb, 0, g)),
        ],
        out_specs=pl.BlockSpec((1, tq, wo), lambda bb, g, i: (bb, i, g)),
        out_shape=jax.ShapeDtypeStruct((b, t, groups * wo), BF16),
        compiler_params=_cparams(("parallel", "parallel", "arbitrary")),
        name="attn_mla" if mla else "attn_gqa",
    )(q, k, v)


def _top2_combine(logits):
    lane = lax.broadcasted_iota(jnp.int32, logits.shape, 1)
    neg = jnp.float32(-jnp.inf)
    lg = jnp.where(lane < N_EXPERTS, logits, neg)
    m1 = jnp.max(lg, axis=-1, keepdims=True)
    i1 = jnp.min(jnp.where(lg == m1, lane, LANE), axis=-1, keepdims=True)
    sel1 = lane == i1
    lg2 = jnp.where(sel1, neg, lg)
    m2 = jnp.max(lg2, axis=-1, keepdims=True)
    i2 = jnp.min(jnp.where(lg2 == m2, lane, LANE), axis=-1, keepdims=True)
    sel2 = lane == i2
    e2 = jnp.exp(m2 - m1)
    w1 = 1.0 / (1.0 + e2)
    w2 = e2 / (1.0 + e2)
    return jnp.where(sel1, w1, 0.0) + jnp.where(sel2, w2, 0.0)


def _post_kernel(*refs, moe):
    if moe:
        (om_ref, og_ref, gate_ref, x_ref, mod_ref, wom_ref, wog_ref, wout_ref, g2_ref,
         rhi_ref, rlo_ref, xo_ref, h2_ref, comb_ref) = refs
    else:
        (om_ref, og_ref, gate_ref, x_ref, mod_ref, wom_ref, wog_ref, wout_ref, g2_ref,
         xo_ref, h2_ref) = refs
    a = _dot(om_ref[0], wom_ref[...])
    bb = _dot(og_ref[0], wog_ref[...])
    gate = gate_ref[0]
    merged = gate[:, :1024].astype(F32) * a + gate[:, 1024:].astype(F32) * bb
    mix = _dot(merged.astype(BF16), wout_ref[...])
    mod = mod_ref[0]
    x1 = x_ref[0] + mod[2:3] * mix
    xo_ref[0] = x1
    h2 = (_rms(x1) * g2_ref[...]) * (1.0 + mod[4:5]) + mod[3:4]
    h2_ref[0] = h2.astype(BF16)
    if moe:
        h_hi = h2.astype(BF16)
        h_lo = (h2 - h_hi.astype(F32)).astype(BF16)
        logits = (_dot(h_hi, rhi_ref[...]) + _dot(h_lo, rhi_ref[...])
                  + _dot(h_hi, rlo_ref[...]))
        comb_ref[0] = _top2_combine(logits)


def _post_call(om, og, gates, xs, mod, lw, *, n_ctx, tm, moe):
    b, t, d = xs.shape
    nct = n_ctx // tm
    ctx_row = b

    def tok(i, bb):
        return (bb, i, 0)

    def modmap(i, bb):
        return (jnp.where(i < nct, ctx_row, bb), 0, 0)

    in_specs = [
        pl.BlockSpec((1, tm, 512), tok),
        pl.BlockSpec((1, tm, 512), tok),
        pl.BlockSpec((1, tm, 2048), tok),
        pl.BlockSpec((1, tm, d), tok),
        pl.BlockSpec((1, 6, d), modmap),
        _const_spec((512, d)),
        _const_spec((512, d)),
        _const_spec((d, d)),
        _const_spec((1, d)),
    ]
    args = [om, og, gates, xs, mod, lw["wom"], lw["wog"], lw["wout"], lw["g2"]]
    out_specs = [pl.BlockSpec((1, tm, d), tok), pl.BlockSpec((1, tm, d), tok)]
    out_shape = [jax.ShapeDtypeStruct((b, t, d), F32), jax.ShapeDtypeStruct((b, t, d), BF16)]
    if moe:
        in_specs += [_const_spec((d, LANE)), _const_spec((d, LANE))]
        args += [lw["r_hi"], lw["r_lo"]]
        out_specs.append(pl.BlockSpec((1, tm, LANE), tok))
        out_shape.append(jax.ShapeDtypeStruct((b, t, LANE), F32))
    return pl.pallas_call(
        functools.partial(_post_kernel, moe=moe),
        grid=(t // tm, b),
        in_specs=in_specs,
        out_specs=out_specs,
        out_shape=out_shape,
        input_output_aliases={3: 0},
        compiler_params=_cparams(("parallel", "arbitrary")),
        name="post_moe" if moe else "post_dense",
    )(*args)


def _swiglu_chunk(hb, w1, w3, w2):
    a = _dot(hb, w1)
    g = (a * jax.nn.sigmoid(a)) * _dot(hb, w3)
    return _dot(g.astype(BF16), w2)


def _dense_ffn_kernel(h_ref, x_ref, mod_ref, w1_ref, w3_ref, w2_ref, o_ref, *, fc):
    hb = h_ref[0]
    f = w1_ref.shape[1]
    acc = None
    for lo in range(0, f, fc):
        part = _swiglu_chunk(hb, w1_ref[:, lo:lo + fc], w3_ref[:, lo:lo + fc],
                             w2_ref[lo:lo + fc, :])
        acc = part if acc is None else acc + part
    o_ref[0] = x_ref[0] + mod_ref[0][5:6] * acc


def _dense_ffn_call(h2, xs, mod, lw, *, n_ctx, tm):
    b, t, d = xs.shape
    f = lw["w1"].shape[1]
    nct = n_ctx // tm
    ctx_row = b

    def tok(i, bb):
        return (bb, i, 0)

    def modmap(i, bb):
        return (jnp.where(i < nct, ctx_row, bb), 0, 0)

    return pl.pallas_call(
        functools.partial(_dense_ffn_kernel, fc=f // 2),
        grid=(t // tm, b),
        in_specs=[
            pl.BlockSpec((1, tm, d), tok),
            pl.BlockSpec((1, tm, d), tok),
            pl.BlockSpec((1, 6, d), modmap),
            pl.BlockSpec((d, f), lambda i, bb: (0, 0), pipeline_mode=pl.Buffered(1)),
            pl.BlockSpec((d, f), lambda i, bb: (0, 0), pipeline_mode=pl.Buffered(1)),
            pl.BlockSpec((f, d), lambda i, bb: (0, 0), pipeline_mode=pl.Buffered(1)),
        ],
        out_specs=pl.BlockSpec((1, tm, d), tok),
        out_shape=jax.ShapeDtypeStruct((b, t, d), F32),
        input_output_aliases={1: 0},
        compiler_params=_cparams(("parallel", "arbitrary")),
        name="ffn_dense",
    )(h2, xs, mod, lw["w1"], lw["w3"], lw["w2"])


def _moe_ffn_kernel(h_ref, x_ref, mod_ref, comb_ref, w1_ref, w3_ref, w2_ref, o_ref, acc_ref):
    e = pl.program_id(2)
    fh = pl.program_id(3)

    @pl.when((e == 0) & (fh == 0))
    def _():
        acc_ref[...] = jnp.zeros_like(acc_ref)

    comb = comb_ref[0]
    lane = lax.broadcasted_iota(jnp.int32, comb.shape, 1)
    we = jnp.sum(jnp.where(lane == e, comb, 0.0), axis=-1, keepdims=True)
    acc_ref[...] += we * _swiglu_chunk(h_ref[0], w1_ref[0], w3_ref[0], w2_ref[0])

    @pl.when((e == pl.num_programs(2) - 1) & (fh == pl.num_programs(3) - 1))
    def _():
        o_ref[0] = x_ref[0] + mod_ref[0][5:6] * acc_ref[...]


def _moe_ffn_call(h2, xs, mod, comb, lw, *, n_ctx, tm):
    b, t, d = xs.shape
    ne, _, f = lw["w1"].shape
    nfh = 2
    fc = f // nfh
    nct = n_ctx // tm
    ctx_row = b

    def tok(i, bb, e, fh):
        return (bb, i, 0)

    def modmap(i, bb, e, fh):
        return (jnp.where(i < nct, ctx_row, bb), 0, 0)

    return pl.pallas_call(
        _moe_ffn_kernel,
        grid=(t // tm, b, ne, nfh),
        in_specs=[
            pl.BlockSpec((1, tm, d), tok),
            pl.BlockSpec((1, tm, d), tok),
            pl.BlockSpec((1, 6, d), modmap),
            pl.BlockSpec((1, tm, LANE), tok),
            pl.BlockSpec((1, d, fc), lambda i, bb, e, fh: (e, 0, fh)),
            pl.BlockSpec((1, d, fc), lambda i, bb, e, fh: (e, 0, fh)),
            pl.BlockSpec((1, fc, d), lambda i, bb, e, fh: (e, fh, 0)),
        ],
        out_specs=pl.BlockSpec((1, tm, d), tok),
        out_shape=jax.ShapeDtypeStruct((b, t, d), F32),
        scratch_shapes=[pltpu.VMEM((tm, d), F32)],
        input_output_aliases={1: 0},
        compiler_params=_cparams(("parallel", "arbitrary", "arbitrary", "arbitrary")),
        name="ffn_moe",
    )(h2, xs, mod, comb, lw["w1"], lw["w3"], lw["w2"])


def _final_norm_kernel(x_ref, g_ref, o_ref):
    o_ref[0] = _rms(x_ref[0]) * g_ref[...]


def _final_norm_call(xs, g, *, n_ctx, tm):
    b, t, d = xs.shape
    off = n_ctx // tm
    n_lat = t - n_ctx
    return pl.pallas_call(
        _final_norm_kernel,
        grid=(n_lat // tm, b),
        in_specs=[pl.BlockSpec((1, tm, d), lambda i, bb: (bb, i + off, 0)),
                  _const_spec((1, d))],
        out_specs=pl.BlockSpec((1, tm, d), lambda i, bb: (bb, i, 0)),
        out_shape=jax.ShapeDtypeStruct((b, n_lat, d), F32),
        compiler_params=_cparams(("parallel", "arbitrary")),
        name="final_norm",
    )(xs, g)


def _rope_parts(n_ctx, n_lat, dim):
    half = dim // 2
    quarter = half // 2
    inv = ROPE_THETA ** (-(jnp.arange(quarter, dtype=F32) * 2.0) / half)
    tpos = jnp.arange(n_lat)
    row = (tpos // GRID_W).astype(F32)
    col = (tpos % GRID_W).astype(F32)
    zeros = jnp.zeros((n_ctx, quarter), F32)
    ang_r = jnp.concatenate([zeros, row[:, None] * inv], axis=0)
    ang_c = jnp.concatenate([zeros, col[:, None] * inv], axis=0)
    cr, sr, cc, sc = jnp.cos(ang_r), jnp.sin(ang_r), jnp.cos(ang_c), jnp.sin(ang_c)
    cos = jnp.concatenate([cr, cr, cc, cc], axis=1)
    sin = jnp.concatenate([-sr, sr, -sc, sc], axis=1)
    return cos, sin


def _swap_perm(dim):
    q = dim // 4
    idx = np.arange(dim)
    return np.concatenate([idx[q:2 * q], idx[:q], idx[3 * q:], idx[2 * q:3 * q]])


def _layer_tables(cos_m, sin_m, cos_g, sin_g, gq, gk):
    t = cos_m.shape[0]
    z64 = jnp.zeros((t, 64), F32)
    z32 = jnp.zeros((t, 32), F32)
    sm = MLA_SCALE * LOG2E
    sg = GQA_SCALE * LOG2E
    pg = _swap_perm(GQA_HEAD_DIM)
    am = jnp.concatenate([jnp.full((t, 64), sm, F32), sm * cos_m, z32], axis=1)
    bm = jnp.concatenate([z64, sm * sin_m, z32], axis=1)
    ck = jnp.concatenate([z64, cos_m, z32], axis=1)
    sk = jnp.concatenate([z64, sin_m, z32], axis=1)
    aq = sg * cos_g * gq[None, :]
    bq = sg * sin_g * gq[pg][None, :]
    ak = cos_g * gk[None, :]
    bk = sin_g * gk[pg][None, :]
    two = lambda a: jnp.concatenate([a, a], axis=1)
    return jnp.stack([am, bm, ck, sk, two(aq), two(bq), two(ak), two(bk)], axis=0)


def _layer_weights(l, w_in, mla_w_qb, mla_w_kvb, w_o_mla, w_o_gqa, w_out):
    d = w_in.shape[1]
    wl = w_in[l]
    offs = np.cumsum([0, MLA_Q_LORA, MLA_KV_LORA, MLA_ROPE, 512, 128, 128, d, d])
    qc, kvc, kpe, qg, kg, vg, gm, gg = [wl[:, offs[i]:offs[i + 1]] for i in range(8)]
    pm = _swap_perm(MLA_ROPE)
    pg = _swap_perm(GQA_HEAD_DIM)
    pg8 = np.concatenate([pg + 64 * h for h in range(GQA_HEADS)])
    pg2 = np.concatenate([pg + 64 * h for h in range(GQA_KV_HEADS)])

    def pad_kpe(w):
        return jnp.pad(w, ((0, 0), (64, 32)))

    def dup(w):
        return jnp.concatenate([w[:, :64], w[:, :64], w[:, 64:], w[:, 64:]], axis=1)

    w_ext = jnp.concatenate(
        [qc, kvc, pad_kpe(kpe), pad_kpe(kpe[:, pm]), qg, qg[:, pg8],
         dup(kg), dup(kg[:, pg2]), dup(vg), gm, gg], axis=1).astype(BF16)

    wqb = mla_w_qb[l].reshape(MLA_Q_LORA, MLA_HEADS, MLA_NOPE + MLA_ROPE)
    rope_sw = wqb[:, :, MLA_NOPE:][:, :, pm]
    wqb_pad = jnp.pad(wqb, ((0, 0), (0, 0), (0, 32))).reshape(MLA_Q_LORA, 1024)
    wqbs_pad = jnp.pad(rope_sw, ((0, 0), (0, 0), (64, 32))).reshape(MLA_Q_LORA, 1024)

    wkvb = mla_w_kvb[l].reshape(MLA_KV_LORA, MLA_HEADS, MLA_NOPE + MLA_V)
    wkk = jnp.pad(wkvb[:, :, :MLA_NOPE], ((0, 0), (0, 0), (0, 64))).reshape(MLA_KV_LORA, 1024)
    wkv = wkvb[:, :, MLA_NOPE:].reshape(MLA_KV_LORA, MLA_HEADS * MLA_V)

    eye = np.kron(np.eye(8, dtype=np.float32), np.full((64, 64), 1.0 / 64, np.float32))
    return {
        "w_in": w_ext,
        "wqb": wqb_pad.astype(BF16),
        "wqbs": wqbs_pad.astype(BF16),
        "wkk": wkk.astype(BF16),
        "wkv": wkv.astype(BF16),
        "bd": jnp.asarray(eye, BF16),
        "wom": w_o_mla[l].astype(BF16),
        "wog": w_o_gqa[l].astype(BF16),
        "wout": w_out[l].astype(BF16),
    }


def kernel(x, c, ctx, c_ctx, w_ada, b_ada, norm1_g, norm2_g, w_in, mla_q_norm_g, mla_w_qb,
           mla_kv_norm_g, mla_w_kvb, gqa_q_norm_g, gqa_k_norm_g, w_o_mla, w_o_gqa, w_out,
           dense_w1, dense_w3, dense_w2, moe_router, moe_w1, moe_w3, moe_w2, final_norm_g):
    b, n_lat, d = x.shape
    n_ctx = ctx.shape[1]
    depth = w_ada.shape[0]
    t = n_ctx + n_lat
    tm = 256
    assert n_ctx % tm == 0 and n_lat % tm == 0 and n_lat % GRID_W == 0

    rows = -(-(b + 1) // 8) * 8
    cc = jnp.zeros((rows, d), F32).at[:b].set(c).at[b].set(c_ctx)
    mod_all = _ada_call(cc, w_ada, b_ada).reshape(depth, rows, 6, d)

    cos_m, sin_m = _rope_parts(n_ctx, n_lat, MLA_ROPE)
    cos_g, sin_g = _rope_parts(n_ctx, n_lat, GQA_HEAD_DIM)

    xs = jnp.concatenate([ctx, x], axis=1)
    for l in range(depth):
        lw = _layer_weights(l, w_in, mla_w_qb, mla_w_kvb, w_o_mla, w_o_gqa, w_out)
        lw["gq"] = mla_q_norm_g[l][None, :]
        lw["gkv"] = mla_kv_norm_g[l][None, :]
        lw["g2"] = norm2_g[l][None, :]
        tabs = _layer_tables(cos_m, sin_m, cos_g, sin_g, gqa_q_norm_g[l], gqa_k_norm_g[l])
        mod = mod_all[l]
        moe = l % 2 == 1

        qm, km, vm, qg, kg, vg, gates = _inproj_call(
            xs, mod, norm1_g[l][None, :], lw, tabs, n_ctx=n_ctx, tm=tm)
        om = _attn_call(qm, km, vm, mla=True, n_ctx=n_ctx, tq=256, tk=256)
        og = _attn_call(qg, kg, vg, mla=False, n_ctx=n_ctx, tq=256, tk=256)

        if moe:
            r = jnp.pad(moe_router[l // 2], ((0, 0), (0, LANE - N_EXPERTS)))
            lw["r_hi"] = r.astype(BF16)
            lw["r_lo"] = (r - lw["r_hi"].astype(F32)).astype(BF16)
            xs, h2, comb = _post_call(om, og, gates, xs, mod, lw, n_ctx=n_ctx, tm=tm, moe=True)
            lw["w1"] = moe_w1[l // 2].astype(BF16)
            lw["w3"] = moe_w3[l // 2].astype(BF16)
            lw["w2"] = moe_w2[l // 2].astype(BF16)
            xs = _moe_ffn_call(h2, xs, mod, comb, lw, n_ctx=n_ctx, tm=tm)
        else:
            xs, h2 = _post_call(om, og, gates, xs, mod, lw, n_ctx=n_ctx, tm=tm, moe=False)
            lw["w1"] = dense_w1[l // 2].astype(BF16)
            lw["w3"] = dense_w3[l // 2].astype(BF16)
            lw["w2"] = dense_w2[l // 2].astype(BF16)
            xs = _dense_ffn_call(h2, xs, mod, lw, n_ctx=n_ctx, tm=tm)

    return _final_norm_call(xs, final_norm_g[None, :], n_ctx=n_ctx, tm=tm)
```

```python
import functools
import math

import numpy as np
import jax
import jax.numpy as jnp
from jax import lax
from jax.experimental import pallas as pl
from jax.experimental.pallas import tpu as pltpu

F32 = jnp.float32
BF16 = jnp.bfloat16

GRID_W = 64
ROPE_THETA = 10000.0
NORM_EPS = 1e-6
MLA_HEADS = 8
MLA_Q_LORA = 384
MLA_KV_LORA = 256
MLA_NOPE = 64
MLA_ROPE = 32
MLA_V = 64
MLA_SCALE = (MLA_NOPE + MLA_ROPE) ** -0.5
GQA_HEADS = 8
GQA_KV_HEADS = 2
GQA_GROUP = GQA_HEADS // GQA_KV_HEADS
GQA_HEAD_DIM = 64
GQA_SCALE = GQA_HEAD_DIM ** -0.5
N_EXPERTS = 8
LOG2E = math.log2(math.e)

LANE = 128
VMEM_LIMIT = 56 * 1024 * 1024

C_QC = 0
C_KVC = C_QC + MLA_Q_LORA
C_KPE = C_KVC + MLA_KV_LORA
C_KPES = C_KPE + LANE
C_QG = C_KPES + LANE
C_QGS = C_QG + 512
C_KG = C_QGS + 512
C_KGS = C_KG + 256
C_VG = C_KGS + 256
C_GATE = C_VG + 256
C_END = C_GATE + 2048

T_AM, T_BM, T_CK, T_SK, T_AQ, T_BQ, T_AK, T_BK = range(8)


def _cparams(sem, flags=None):
    return pltpu.CompilerParams(dimension_semantics=sem, vmem_limit_bytes=VMEM_LIMIT,
                                flags=flags)


ATTN_FLAGS = None


def _const_spec(shape):
    nd = len(shape)
    return pl.BlockSpec(shape, lambda *_: (0,) * nd)


def _dot(a, b):
    return jnp.dot(a, b, preferred_element_type=F32)


def _rms(x):
    return x * lax.rsqrt(jnp.mean(x * x, axis=-1, keepdims=True) + NORM_EPS)


def _ada_kernel(cc_ref, w_ref, b_ref, o_ref):
    a = cc_ref[...]
    a = a * jax.nn.sigmoid(a)
    a_hi = a.astype(BF16)
    a_lo = (a - a_hi.astype(F32)).astype(BF16)
    w = w_ref[0]
    w_hi = w.astype(BF16)
    w_lo = (w - w_hi.astype(F32)).astype(BF16)
    acc = _dot(a_hi, w_hi) + _dot(a_lo, w_hi) + _dot(a_hi, w_lo)
    o_ref[0] = acc + b_ref[0]


def _ada_call(cc, w_ada, b_ada):
    depth, d, n = w_ada.shape
    rows = cc.shape[0]
    tn = 1536
    return pl.pallas_call(
        _ada_kernel,
        grid=(depth, n // tn),
        in_specs=[
            pl.BlockSpec((rows, d), lambda l, j: (0, 0)),
            pl.BlockSpec((1, d, tn), lambda l, j: (l, 0, j)),
            pl.BlockSpec((1, 1, tn), lambda l, j: (l, 0, j)),
        ],
        out_specs=pl.BlockSpec((1, rows, tn), lambda l, j: (l, 0, j)),
        out_shape=jax.ShapeDtypeStruct((depth, rows, n), F32),
        compiler_params=_cparams(("arbitrary", "arbitrary")),
        name="ada",
    )(cc, w_ada, b_ada.reshape(depth, 1, n))


def _group_mean_sq(x, bd):
    x2 = x * x
    hi = x2.astype(BF16)
    lo = (x2 - hi.astype(F32)).astype(BF16)
    return _dot(hi, bd) + _dot(lo, bd)


def _tile_lanes(t, n):
    return jnp.concatenate([t] * n, axis=1)


def _inproj_kernel(x_ref, mod_ref, g1_ref, win_ref, gq_ref, wqb_ref, wqbs_ref, gkv_ref,
                   wkk_ref, wkv_ref, bd_ref, tab_ref,
                   qm_ref, km_ref, vm_ref, qg_ref, kg_ref, vg_ref, gate_ref):
    x = x_ref[0]
    mod = mod_ref[0]
    h = (_rms(x) * g1_ref[...]) * (1.0 + mod[1:2]) + mod[0:1]
    hb = h.astype(BF16)

    def proj(lo, hi):
        return _dot(hb, win_ref[:, lo:hi])

    qcn = (_rms(proj(C_QC, C_KVC)) * gq_ref[...]).astype(BF16)
    q = _dot(qcn, wqb_ref[...])
    qs = _dot(qcn, wqbs_ref[...])
    am = _tile_lanes(tab_ref[T_AM], MLA_HEADS)
    bm = _tile_lanes(tab_ref[T_BM], MLA_HEADS)
    qm_ref[0] = (q * am + qs * bm).astype(BF16)

    kvn = (_rms(proj(C_KVC, C_KPE)) * gkv_ref[...]).astype(BF16)
    kpe = proj(C_KPE, C_KPES) * tab_ref[T_CK] + proj(C_KPES, C_QG) * tab_ref[T_SK]
    kn = _dot(kvn, wkk_ref[...])
    km_ref[0] = (kn + _tile_lanes(kpe, MLA_HEADS)).astype(BF16)
    vm_ref[0] = _dot(kvn, wkv_ref[...]).astype(BF16)

    bd = bd_ref[...]
    qg = proj(C_QG, C_QGS)
    qgs = proj(C_QGS, C_KG)
    rq = lax.rsqrt(_group_mean_sq(qg, bd) + NORM_EPS)
    aq = _tile_lanes(tab_ref[T_AQ], 4)
    bq = _tile_lanes(tab_ref[T_BQ], 4)
    qg_ref[0] = (rq * (qg * aq + qgs * bq)).astype(BF16)

    kg = proj(C_KG, C_KGS)
    kgs = proj(C_KGS, C_VG)
    rk = lax.rsqrt(_group_mean_sq(kg, bd[:256, :256]) + NORM_EPS)
    ak = _tile_lanes(tab_ref[T_AK], 2)
    bk = _tile_lanes(tab_ref[T_BK], 2)
    kg_ref[0] = (rk * (kg * ak + kgs * bk)).astype(BF16)
    vg_ref[0] = proj(C_VG, C_GATE).astype(BF16)

    gate_ref[0] = jax.nn.sigmoid(proj(C_GATE, C_END)).astype(BF16)


def _inproj_call(xs, mod, g1, lw, tabs, *, n_ctx, tm):
    b, t, d = xs.shape
    nlt = (t - n_ctx) // tm
    ctx_row = b

    def tok(i, bb):
        return (bb, i, 0)

    def modmap(i, bb):
        return (jnp.where(i >= nlt, ctx_row, bb), 0, 0)

    out_w = (1024, 1024, 512, 512, 256, 256, 2048)
    return pl.pallas_call(
        _inproj_kernel,
        grid=(t // tm, b),
        in_specs=[
            pl.BlockSpec((1, tm, d), tok),
            pl.BlockSpec((1, 6, d), modmap),
            _const_spec((1, d)),
            _const_spec((d, C_END)),
            _const_spec((1, MLA_Q_LORA)),
            _const_spec((MLA_Q_LORA, 1024)),
            _const_spec((MLA_Q_LORA, 1024)),
            _const_spec((1, MLA_KV_LORA)),
            _const_spec((MLA_KV_LORA, 1024)),
            _const_spec((MLA_KV_LORA, 512)),
            _const_spec((512, 512)),
            pl.BlockSpec((8, tm, LANE), lambda i, bb: (0, i, 0)),
        ],
        out_specs=[pl.BlockSpec((1, tm, w), tok) for w in out_w],
        out_shape=[jax.ShapeDtypeStruct((b, t, w), BF16) for w in out_w],
        compiler_params=_cparams(("parallel", "arbitrary")),
        name="inproj",
    )(xs, mod, g1, lw["w_in"], lw["gq"], lw["wqb"], lw["wqbs"], lw["gkv"],
      lw["wkk"], lw["wkv"], lw["bd"], tabs)


def _fold_lanes(x, op):
    out = x[:, :LANE]
    for lo in range(LANE, x.shape[1], LANE):
        out = op(out, x[:, lo:lo + LANE])
    return out


def _key_block(nkeys):
    for kb in (768, 512, 256):
        if nkeys % kb == 0:
            return kb
    raise ValueError(nkeys)


def _mask_gqa_q(q):
    lane = lax.broadcasted_iota(jnp.int32, q.shape, 1)
    keep = (lane // 64) == (pl.program_id(1) % 2)
    return jnp.where(keep, q, jnp.zeros_like(q))


def _attn_stages(nkeys, k_ref, v_ref, scores=None, probs=None, values=None):
    kb = _key_block(nkeys)
    mrun = lrun = acc = None
    if probs is not None:
        m = jnp.max(probs[2][...], axis=-1, keepdims=True)
    if values is not None:
        l = jnp.sum(values[1][...], axis=-1, keepdims=True)
    for lo in range(0, nkeys, kb):
        if scores is not None:
            q, s_ref, _ = scores
            s = lax.dot_general(q, k_ref[0, lo:lo + kb, :], (((1,), (1,)), ((), ())),
                                preferred_element_type=F32)
            s_ref[:, lo:lo + kb] = s
            f = _fold_lanes(s, jnp.maximum)
            mrun = f if mrun is None else jnp.maximum(mrun, f)
        if probs is not None:
            s_ref, p_ref, _, _ = probs
            p = jnp.exp2(s_ref[:, lo:lo + kb] - m)
            f = _fold_lanes(p, jnp.add)
            lrun = f if lrun is None else lrun + f
            p_ref[:, lo:lo + kb] = p.astype(BF16)
        if values is not None:
            part = _dot(values[0][:, lo:lo + kb], v_ref[0, lo:lo + kb, :])
            acc = part if acc is None else acc + part
    if scores is not None:
        scores[2][...] = mrun
    if probs is not None:
        probs[3][...] = lrun
    if values is not None:
        values[2]((acc / l).astype(BF16))


def _attn_ctx_kernel(q_ref, k_ref, v_ref, o_ref, s_scr, p_scr, m_scr, l_scr, *, gqa):
    q = q_ref[0]
    if gqa:
        q = _mask_gqa_q(q)
    nkeys = k_ref.shape[1]

    def store(o):
        o_ref[0] = o

    _attn_stages(nkeys, k_ref, v_ref, scores=(q, s_scr, m_scr))
    _attn_stages(nkeys, k_ref, v_ref, probs=(s_scr, p_scr, m_scr, l_scr))
    _attn_stages(nkeys, k_ref, v_ref, values=(p_scr, l_scr, store))


def _attn_lat_kernel(q_ref, k_ref, v_ref, oin_ref, o_ref, s0, s1, p0, p1, m0, m1, l0, l1,
                     *, gqa, tq, nkeys):
    del oin_ref
    j = pl.program_id(2)
    last = pl.num_programs(2) - 1
    s_scr, p_scr, m_scr, l_scr = (s0, s1), (p0, p1), (m0, m1), (l0, l1)

    def scores(slot):
        q = q_ref[0, slot * tq:(slot + 1) * tq, :]
        if gqa:
            q = _mask_gqa_q(q)
        return (q, s_scr[slot], m_scr[slot])

    def probs(slot):
        return (s_scr[slot], p_scr[slot], m_scr[slot], l_scr[slot])

    def values(slot):
        def store(o):
            o_ref[0, slot * tq:(slot + 1) * tq, :] = o
        return (p_scr[slot], l_scr[slot], store)

    stages = functools.partial(_attn_stages, nkeys, k_ref, v_ref)

    @pl.when(j == 0)
    def _():
        stages(scores=scores(0))
        stages(scores=scores(1), probs=probs(0))

    @pl.when((j > 0) & (j < last))
    def _():
        stages(scores=scores(0))
        stages(probs=probs(1))
        stages(values=values(0))
        stages(scores=scores(1))
        stages(probs=probs(0))
        stages(values=values(1))

    @pl.when(j == last)
    def _():
        stages(probs=probs(1), values=values(0))
        stages(values=values(1))


def _attn_call(q, k, v, *, gqa, n_ctx, tq):
    b, t, _ = q.shape
    n_lat = t - n_ctx
    heads = GQA_HEADS if gqa else MLA_HEADS
    if gqa:
        qcol = lambda h: h // 2
        kcol = lambda h: h // GQA_GROUP
        vcol = kcol
    else:
        qcol = lambda h: h
        kcol = lambda h: h
        vcol = lambda h: h // 2
    out_shape = jax.ShapeDtypeStruct((b, t, heads * LANE), BF16)
    sem = ("parallel", "parallel", "arbitrary")

    cblk = n_lat // n_ctx
    o_ctx = pl.pallas_call(
        functools.partial(_attn_ctx_kernel, gqa=gqa),
        grid=(b, heads),
        in_specs=[
            pl.BlockSpec((1, n_ctx, LANE), lambda bb, h: (bb, cblk, qcol(h))),
            pl.BlockSpec((1, n_ctx, LANE), lambda bb, h: (bb, cblk, kcol(h))),
            pl.BlockSpec((1, n_ctx, LANE), lambda bb, h: (bb, cblk, vcol(h))),
        ],
        out_specs=pl.BlockSpec((1, n_ctx, LANE), lambda bb, h: (bb, cblk, h)),
        out_shape=out_shape,
        scratch_shapes=[pltpu.VMEM((n_ctx, n_ctx), F32), pltpu.VMEM((n_ctx, n_ctx), BF16),
                        pltpu.VMEM((n_ctx, LANE), F32), pltpu.VMEM((n_ctx, LANE), F32)],
        compiler_params=_cparams(sem[:2]),
        name="attn_ctx_gqa" if gqa else "attn_ctx_mla",
    )(q, k, v)

    npairs = n_lat // (2 * tq)
    kern = functools.partial(_attn_lat_kernel, gqa=gqa, tq=tq, nkeys=t)
    return pl.pallas_call(
        kern,
        grid=(b, heads, npairs + 1),
        in_specs=[
            pl.BlockSpec((1, 2 * tq, LANE),
                         lambda bb, h, j: (bb, jnp.minimum(j, npairs - 1), qcol(h))),
            pl.BlockSpec((1, t, LANE), lambda bb, h, j: (bb, 0, kcol(h))),
            pl.BlockSpec((1, t, LANE), lambda bb, h, j: (bb, 0, vcol(h))),
            pl.BlockSpec(memory_space=pl.ANY),
        ],
        out_specs=pl.BlockSpec((1, 2 * tq, LANE),
                               lambda bb, h, j: (bb, jnp.maximum(j - 1, 0), h)),
        out_shape=out_shape,
        scratch_shapes=[pltpu.VMEM((tq, t), F32)] * 2 + [pltpu.VMEM((tq, t), BF16)] * 2
        + [pltpu.VMEM((tq, LANE), F32)] * 4,
        input_output_aliases={3: 0},
        compiler_params=_cparams(sem, ATTN_FLAGS),
        name="attn_gqa" if gqa else "attn_mla",
    )(q, k, v, o_ctx)


def _top2_combine(logits):
    lane = lax.broadcasted_iota(jnp.int32, logits.shape, 1)
    neg = jnp.float32(-jnp.inf)
    lg = jnp.where(lane < N_EXPERTS, logits, neg)
    m1 = jnp.max(lg, axis=-1, keepdims=True)
    i1 = jnp.min(jnp.where(lg == m1, lane, LANE), axis=-1, keepdims=True)
    sel1 = lane == i1
    lg2 = jnp.where(sel1, neg, lg)
    m2 = jnp.max(lg2, axis=-1, keepdims=True)
    i2 = jnp.min(jnp.where(lg2 == m2, lane, LANE), axis=-1, keepdims=True)
    sel2 = lane == i2
    e2 = jnp.exp(m2 - m1)
    w1 = 1.0 / (1.0 + e2)
    w2 = e2 / (1.0 + e2)
    return jnp.where(sel1, w1, 0.0) + jnp.where(sel2, w2, 0.0)


def _post_kernel(*refs, moe):
    if moe:
        (om_ref, og_ref, gate_ref, x_ref, mod_ref, wom_ref, wog_ref, wout_ref, g2_ref,
         rhi_ref, rlo_ref, xo_ref, h2_ref, comb_ref) = refs
    else:
        (om_ref, og_ref, gate_ref, x_ref, mod_ref, wom_ref, wog_ref, wout_ref, g2_ref,
         xo_ref, h2_ref) = refs
    a = _dot(om_ref[0], wom_ref[...])
    bb = _dot(og_ref[0], wog_ref[...])
    gate = gate_ref[0]
    merged = gate[:, :1024].astype(F32) * a + gate[:, 1024:].astype(F32) * bb
    mix = _dot(merged.astype(BF16), wout_ref[...])
    mod = mod_ref[0]
    x1 = x_ref[0] + mod[2:3] * mix
    xo_ref[0] = x1
    h2 = (_rms(x1) * g2_ref[...]) * (1.0 + mod[4:5]) + mod[3:4]
    h2_ref[0] = h2.astype(BF16)
    if moe:
        h_hi = h2.astype(BF16)
        h_lo = (h2 - h_hi.astype(F32)).astype(BF16)
        logits = (_dot(h_hi, rhi_ref[...]) + _dot(h_lo, rhi_ref[...])
                  + _dot(h_hi, rlo_ref[...]))
        comb_ref[0] = _top2_combine(logits)


def _post_call(om, og, gates, xs, mod, lw, *, n_ctx, tm, moe):
    b, t, d = xs.shape
    nlt = (t - n_ctx) // tm
    ctx_row = b

    def tok(i, bb):
        return (bb, i, 0)

    def modmap(i, bb):
        return (jnp.where(i >= nlt, ctx_row, bb), 0, 0)

    in_specs = [
        pl.BlockSpec((1, tm, MLA_HEADS * LANE), tok),
        pl.BlockSpec((1, tm, GQA_HEADS * LANE), tok),
        pl.BlockSpec((1, tm, 2048), tok),
        pl.BlockSpec((1, tm, d), tok),
        pl.BlockSpec((1, 6, d), modmap),
        _const_spec((MLA_HEADS * LANE, d)),
        _const_spec((GQA_HEADS * LANE, d)),
        _const_spec((d, d)),
        _const_spec((1, d)),
    ]
    args = [om, og, gates, xs, mod, lw["wom"], lw["wog"], lw["wout"], lw["g2"]]
    out_specs = [pl.BlockSpec((1, tm, d), tok), pl.BlockSpec((1, tm, d), tok)]
    out_shape = [jax.ShapeDtypeStruct((b, t, d), F32), jax.ShapeDtypeStruct((b, t, d), BF16)]
    if moe:
        in_specs += [_const_spec((d, LANE)), _const_spec((d, LANE))]
        args += [lw["r_hi"], lw["r_lo"]]
        out_specs.append(pl.BlockSpec((1, tm, LANE), tok))
        out_shape.append(jax.ShapeDtypeStruct((b, t, LANE), F32))
    return pl.pallas_call(
        functools.partial(_post_kernel, moe=moe),
        grid=(t // tm, b),
        in_specs=in_specs,
        out_specs=out_specs,
        out_shape=out_shape,
        input_output_aliases={3: 0},
        compiler_params=_cparams(("parallel", "arbitrary")),
        name="post_moe" if moe else "post_dense",
    )(*args)


def _swiglu_chunk(hb, w1, w3, w2):
    a = _dot(hb, w1)
    g = (a * jax.nn.sigmoid(a)) * _dot(hb, w3)
    return _dot(g.astype(BF16), w2)


def _dense_ffn_kernel(h_ref, x_ref, mod_ref, w1_ref, w3_ref, w2_ref, o_ref, *, fc):
    hb = h_ref[0]
    f = w1_ref.shape[1]
    acc = None
    for lo in range(0, f, fc):
        part = _swiglu_chunk(hb, w1_ref[:, lo:lo + fc], w3_ref[:, lo:lo + fc],
                             w2_ref[lo:lo + fc, :])
        acc = part if acc is None else acc + part
    o_ref[0] = x_ref[0] + mod_ref[0][5:6] * acc


def _dense_ffn_call(h2, xs, mod, lw, *, n_ctx, tm):
    b, t, d = xs.shape
    f = lw["w1"].shape[1]
    nlt = (t - n_ctx) // tm
    ctx_row = b

    def tok(i, bb):
        return (bb, i, 0)

    def modmap(i, bb):
        return (jnp.where(i >= nlt, ctx_row, bb), 0, 0)

    return pl.pallas_call(
        functools.partial(_dense_ffn_kernel, fc=f // 2),
        grid=(t // tm, b),
        in_specs=[
            pl.BlockSpec((1, tm, d), tok),
            pl.BlockSpec((1, tm, d), tok),
            pl.BlockSpec((1, 6, d), modmap),
            pl.BlockSpec((d, f), lambda i, bb: (0, 0), pipeline_mode=pl.Buffered(1)),
            pl.BlockSpec((d, f), lambda i, bb: (0, 0), pipeline_mode=pl.Buffered(1)),
            pl.BlockSpec((f, d), lambda i, bb: (0, 0), pipeline_mode=pl.Buffered(1)),
        ],
        out_specs=pl.BlockSpec((1, tm, d), tok),
        out_shape=jax.ShapeDtypeStruct((b, t, d), F32),
        input_output_aliases={1: 0},
        compiler_params=_cparams(("parallel", "arbitrary")),
        name="ffn_dense",
    )(h2, xs, mod, lw["w1"], lw["w3"], lw["w2"])


def _moe_ffn_kernel(h_ref, x_ref, mod_ref, comb_ref, w1_ref, w3_ref, w2_ref, o_ref, acc_ref):
    e = pl.program_id(2)
    fh = pl.program_id(3)

    @pl.when((e == 0) & (fh == 0))
    def _():
        acc_ref[...] = jnp.zeros_like(acc_ref)

    comb = comb_ref[0]
    lane = lax.broadcasted_iota(jnp.int32, comb.shape, 1)
    we = jnp.sum(jnp.where(lane == e, comb, 0.0), axis=-1, keepdims=True)
    acc_ref[...] += we * _swiglu_chunk(h_ref[0], w1_ref[0], w3_ref[0], w2_ref[0])

    @pl.when((e == pl.num_programs(2) - 1) & (fh == pl.num_programs(3) - 1))
    def _():
        o_ref[0] = x_ref[0] + mod_ref[0][5:6] * acc_ref[...]


def _moe_ffn_call(h2, xs, mod, comb, lw, *, n_ctx, tm):
    b, t, d = xs.shape
    ne, _, f = lw["w1"].shape
    nfh = 2
    fc = f // nfh
    nlt = (t - n_ctx) // tm
    ctx_row = b

    def tok(i, bb, e, fh):
        return (bb, i, 0)

    def modmap(i, bb, e, fh):
        return (jnp.where(i >= nlt, ctx_row, bb), 0, 0)

    return pl.pallas_call(
        _moe_ffn_kernel,
        grid=(t // tm, b, ne, nfh),
        in_specs=[
            pl.BlockSpec((1, tm, d), tok),
            pl.BlockSpec((1, tm, d), tok),
            pl.BlockSpec((1, 6, d), modmap),
            pl.BlockSpec((1, tm, LANE), tok),
            pl.BlockSpec((1, d, fc), lambda i, bb, e, fh: (e, 0, fh)),
            pl.BlockSpec((1, d, fc), lambda i, bb, e, fh: (e, 0, fh)),
            pl.BlockSpec((1, fc, d), lambda i, bb, e, fh: (e, fh, 0)),
        ],
        out_specs=pl.BlockSpec((1, tm, d), tok),
        out_shape=jax.ShapeDtypeStruct((b, t, d), F32),
        scratch_shapes=[pltpu.VMEM((tm, d), F32)],
        input_output_aliases={1: 0},
        compiler_params=_cparams(("parallel", "arbitrary", "arbitrary", "arbitrary")),
        name="ffn_moe",
    )(h2, xs, mod, comb, lw["w1"], lw["w3"], lw["w2"])


def _final_norm_kernel(x_ref, g_ref, o_ref):
    o_ref[0] = _rms(x_ref[0]) * g_ref[...]


def _final_norm_call(xs, g, *, n_ctx, tm):
    b, t, d = xs.shape
    n_lat = t - n_ctx
    return pl.pallas_call(
        _final_norm_kernel,
        grid=(n_lat // tm, b),
        in_specs=[pl.BlockSpec((1, tm, d), lambda i, bb: (bb, i, 0)),
                  _const_spec((1, d))],
        out_specs=pl.BlockSpec((1, tm, d), lambda i, bb: (bb, i, 0)),
        out_shape=jax.ShapeDtypeStruct((b, n_lat, d), F32),
        compiler_params=_cparams(("parallel", "arbitrary")),
        name="final_norm",
    )(xs, g)


def _rope_parts(n_ctx, n_lat, dim):
    half = dim // 2
    quarter = half // 2
    inv = ROPE_THETA ** (-(jnp.arange(quarter, dtype=F32) * 2.0) / half)
    tpos = jnp.arange(n_lat)
    row = (tpos // GRID_W).astype(F32)
    col = (tpos % GRID_W).astype(F32)
    zeros = jnp.zeros((n_ctx, quarter), F32)
    ang_r = jnp.concatenate([row[:, None] * inv, zeros], axis=0)
    ang_c = jnp.concatenate([col[:, None] * inv, zeros], axis=0)
    cr, sr, cc, sc = jnp.cos(ang_r), jnp.sin(ang_r), jnp.cos(ang_c), jnp.sin(ang_c)
    cos = jnp.concatenate([cr, cr, cc, cc], axis=1)
    sin = jnp.concatenate([-sr, sr, -sc, sc], axis=1)
    return cos, sin


def _swap_perm(dim):
    q = dim // 4
    idx = np.arange(dim)
    return np.concatenate([idx[q:2 * q], idx[:q], idx[3 * q:], idx[2 * q:3 * q]])


def _layer_tables(cos_m, sin_m, cos_g, sin_g, gq, gk):
    t = cos_m.shape[0]
    z64 = jnp.zeros((t, 64), F32)
    z32 = jnp.zeros((t, 32), F32)
    sm = MLA_SCALE * LOG2E
    sg = GQA_SCALE * LOG2E
    pg = _swap_perm(GQA_HEAD_DIM)
    am = jnp.concatenate([jnp.full((t, 64), sm, F32), sm * cos_m, z32], axis=1)
    bm = jnp.concatenate([z64, sm * sin_m, z32], axis=1)
    ck = jnp.concatenate([z64, cos_m, z32], axis=1)
    sk = jnp.concatenate([z64, sin_m, z32], axis=1)
    aq = sg * cos_g * gq[None, :]
    bq = sg * sin_g * gq[pg][None, :]
    ak = cos_g * gk[None, :]
    bk = sin_g * gk[pg][None, :]
    two = lambda a: jnp.concatenate([a, a], axis=1)
    return jnp.stack([am, bm, ck, sk, two(aq), two(bq), two(ak), two(bk)], axis=0)


def _layer_weights(l, w_in, mla_w_qb, mla_w_kvb, w_o_mla, w_o_gqa, w_out):
    d = w_in.shape[1]
    wl = w_in[l]
    offs = np.cumsum([0, MLA_Q_LORA, MLA_KV_LORA, MLA_ROPE, 512, 128, 128, d, d])
    qc, kvc, kpe, qg, kg, vg, gm, gg = [wl[:, offs[i]:offs[i + 1]] for i in range(8)]
    pm = _swap_perm(MLA_ROPE)
    pg = _swap_perm(GQA_HEAD_DIM)
    pg8 = np.concatenate([pg + 64 * h for h in range(GQA_HEADS)])
    pg2 = np.concatenate([pg + 64 * h for h in range(GQA_KV_HEADS)])

    def pad_kpe(w):
        return jnp.pad(w, ((0, 0), (64, 32)))

    def dup(w):
        return jnp.concatenate([w[:, :64], w[:, :64], w[:, 64:], w[:, 64:]], axis=1)

    w_ext = jnp.concatenate(
        [qc, kvc, pad_kpe(kpe), pad_kpe(kpe[:, pm]), qg, qg[:, pg8],
         dup(kg), dup(kg[:, pg2]), dup(vg), gm, gg], axis=1).astype(BF16)

    wqb = mla_w_qb[l].reshape(MLA_Q_LORA, MLA_HEADS, MLA_NOPE + MLA_ROPE)
    rope_sw = wqb[:, :, MLA_NOPE:][:, :, pm]
    wqb_pad = jnp.pad(wqb, ((0, 0), (0, 0), (0, 32))).reshape(MLA_Q_LORA, 1024)
    wqbs_pad = jnp.pad(rope_sw, ((0, 0), (0, 0), (64, 32))).reshape(MLA_Q_LORA, 1024)

    wkvb = mla_w_kvb[l].reshape(MLA_KV_LORA, MLA_HEADS, MLA_NOPE + MLA_V)
    wkk = jnp.pad(wkvb[:, :, :MLA_NOPE], ((0, 0), (0, 0), (0, 64))).reshape(MLA_KV_LORA, 1024)
    wkv = wkvb[:, :, MLA_NOPE:].reshape(MLA_KV_LORA, MLA_HEADS * MLA_V)

    zero = jnp.zeros((MLA_HEADS // 2, 64, d), F32)
    wom = w_o_mla[l].reshape(MLA_HEADS // 2, 2, 64, d)
    wom = jnp.stack([wom[:, 0], zero, zero, wom[:, 1]], axis=1).reshape(MLA_HEADS * LANE, d)
    wog = jnp.pad(w_o_gqa[l].reshape(GQA_HEADS, 64, d), ((0, 0), (0, 64), (0, 0)))
    wog = wog.reshape(GQA_HEADS * LANE, d)

    eye = np.kron(np.eye(8, dtype=np.float32), np.full((64, 64), 1.0 / 64, np.float32))
    return {
        "w_in": w_ext,
        "wqb": wqb_pad.astype(BF16),
        "wqbs": wqbs_pad.astype(BF16),
        "wkk": wkk.astype(BF16),
        "wkv": wkv.astype(BF16),
        "bd": jnp.asarray(eye, BF16),
        "wom": wom.astype(BF16),
        "wog": wog.astype(BF16),
        "wout": w_out[l].astype(BF16),
    }


def kernel(x, c, ctx, c_ctx, w_ada, b_ada, norm1_g, norm2_g, w_in, mla_q_norm_g, mla_w_qb,
           mla_kv_norm_g, mla_w_kvb, gqa_q_norm_g, gqa_k_norm_g, w_o_mla, w_o_gqa, w_out,
           dense_w1, dense_w3, dense_w2, moe_router, moe_w1, moe_w3, moe_w2, final_norm_g):
    b, n_lat, d = x.shape
    n_ctx = ctx.shape[1]
    depth = w_ada.shape[0]
    t = n_ctx + n_lat
    tm = 256
    assert n_ctx % tm == 0 and n_lat % tm == 0 and n_lat % GRID_W == 0

    rows = -(-(b + 1) // 8) * 8
    cc = jnp.zeros((rows, d), F32).at[:b].set(c).at[b].set(c_ctx)
    mod_all = _ada_call(cc, w_ada, b_ada).reshape(depth, rows, 6, d)

    cos_m, sin_m = _rope_parts(n_ctx, n_lat, MLA_ROPE)
    cos_g, sin_g = _rope_parts(n_ctx, n_lat, GQA_HEAD_DIM)

    xs = jnp.concatenate([x, ctx], axis=1)
    for l in range(depth):
        lw = _layer_weights(l, w_in, mla_w_qb, mla_w_kvb, w_o_mla, w_o_gqa, w_out)
        lw["gq"] = mla_q_norm_g[l][None, :]
        lw["gkv"] = mla_kv_norm_g[l][None, :]
        lw["g2"] = norm2_g[l][None, :]
        tabs = _layer_tables(cos_m, sin_m, cos_g, sin_g, gqa_q_norm_g[l], gqa_k_norm_g[l])
        mod = mod_all[l]
        moe = l % 2 == 1

        qm, km, vm, qg, kg, vg, gates = _inproj_call(
            xs, mod, norm1_g[l][None, :], lw, tabs, n_ctx=n_ctx, tm=tm)
        om = _attn_call(qm, km, vm, gqa=False, n_ctx=n_ctx, tq=256)
        og = _attn_call(qg, kg, vg, gqa=True, n_ctx=n_ctx, tq=256)

        if moe:
            r = jnp.pad(moe_router[l // 2], ((0, 0), (0, LANE - N_EXPERTS)))
            lw["r_hi"] = r.astype(BF16)
            lw["r_lo"] = (r - lw["r_hi"].astype(F32)).astype(BF16)
            xs, h2, comb = _post_call(om, og, gates, xs, mod, lw, n_ctx=n_ctx, tm=tm, moe=True)
            lw["w1"] = moe_w1[l // 2].astype(BF16)
            lw["w3"] = moe_w3[l // 2].astype(BF16)
            lw["w2"] = moe_w2[l // 2].astype(BF16)
            xs = _moe_ffn_call(h2, xs, mod, comb, lw, n_ctx=n_ctx, tm=tm)
        else:
            xs, h2 = _post_call(om, og, gates, xs, mod, lw, n_ctx=n_ctx, tm=tm, moe=False)
            lw["w1"] = dense_w1[l // 2].astype(BF16)
            lw["w3"] = dense_w3[l // 2].astype(BF16)
            lw["w2"] = dense_w2[l // 2].astype(BF16)
            xs = _dense_ffn_call(h2, xs, mod, lw, n_ctx=n_ctx, tm=tm)

    return _final_norm_call(xs, final_norm_g[None, :], n_ctx=n_ctx, tm=tm)
```

```python
import functools
import math

import numpy as np
import jax
import jax.numpy as jnp
from jax import lax
from jax.experimental import pallas as pl
from jax.experimental.pallas import tpu as pltpu

F32 = jnp.float32
BF16 = jnp.bfloat16

GRID_W = 64
ROPE_THETA = 10000.0
NORM_EPS = 1e-6
MLA_HEADS = 8
MLA_Q_LORA = 384
MLA_KV_LORA = 256
MLA_NOPE = 64
MLA_ROPE = 32
MLA_V = 64
MLA_SCALE = (MLA_NOPE + MLA_ROPE) ** -0.5
GQA_HEADS = 8
GQA_KV_HEADS = 2
GQA_GROUP = GQA_HEADS // GQA_KV_HEADS
GQA_HEAD_DIM = 64
GQA_SCALE = GQA_HEAD_DIM ** -0.5
N_EXPERTS = 8
LOG2E = math.log2(math.e)

LANE = 128
VMEM_LIMIT = 56 * 1024 * 1024

C_QC = 0
C_KVC = C_QC + MLA_Q_LORA
C_KPE = C_KVC + MLA_KV_LORA
C_KPES = C_KPE + LANE
C_QG = C_KPES + LANE
C_QGS = C_QG + 512
C_KG = C_QGS + 512
C_KGS = C_KG + 256
C_VG = C_KGS + 256
C_GATE = C_VG + 256
C_END = C_GATE + 2048

T_AM, T_BM, T_CK, T_SK, T_AQ, T_BQ, T_AK, T_BK = range(8)


def _cparams(sem, flags=None):
    return pltpu.CompilerParams(dimension_semantics=sem, vmem_limit_bytes=VMEM_LIMIT,
                                flags=flags)


ATTN_FLAGS = None


def _const_spec(shape):
    nd = len(shape)
    return pl.BlockSpec(shape, lambda *_: (0,) * nd)


def _dot(a, b):
    return jnp.dot(a, b, preferred_element_type=F32)


def _rms(x):
    return x * lax.rsqrt(jnp.mean(x * x, axis=-1, keepdims=True) + NORM_EPS)


def _ada_kernel(cc_ref, w_ref, b_ref, o_ref):
    a = cc_ref[...]
    a = a * jax.nn.sigmoid(a)
    a_hi = a.astype(BF16)
    a_lo = (a - a_hi.astype(F32)).astype(BF16)
    w = w_ref[0]
    w_hi = w.astype(BF16)
    w_lo = (w - w_hi.astype(F32)).astype(BF16)
    acc = _dot(a_hi, w_hi) + _dot(a_lo, w_hi) + _dot(a_hi, w_lo)
    o_ref[0] = acc + b_ref[0]


def _ada_call(cc, w_ada, b_ada):
    depth, d, n = w_ada.shape
    rows = cc.shape[0]
    tn = 1536
    return pl.pallas_call(
        _ada_kernel,
        grid=(depth, n // tn),
        in_specs=[
            pl.BlockSpec((rows, d), lambda l, j: (0, 0)),
            pl.BlockSpec((1, d, tn), lambda l, j: (l, 0, j)),
            pl.BlockSpec((1, 1, tn), lambda l, j: (l, 0, j)),
        ],
        out_specs=pl.BlockSpec((1, rows, tn), lambda l, j: (l, 0, j)),
        out_shape=jax.ShapeDtypeStruct((depth, rows, n), F32),
        compiler_params=_cparams(("arbitrary", "arbitrary")),
        name="ada",
    )(cc, w_ada, b_ada.reshape(depth, 1, n))


def _group_mean_sq(x, bd):
    x2 = x * x
    hi = x2.astype(BF16)
    lo = (x2 - hi.astype(F32)).astype(BF16)
    return _dot(hi, bd) + _dot(lo, bd)


def _tile_lanes(t, n):
    return jnp.concatenate([t] * n, axis=1)


def _with_ones(v):
    ones = jnp.ones((v.shape[0], LANE), v.dtype)
    parts = []
    for lo in range(0, v.shape[1], LANE):
        parts += [v[:, lo:lo + LANE], ones]
    return jnp.concatenate(parts, axis=1)


def _inproj_kernel(x_ref, mod_ref, g1_ref, win_ref, gq_ref, wqb_ref, wqbs_ref, gkv_ref,
                   wkk_ref, wkv_ref, bd_ref, tab_ref,
                   qm_ref, km_ref, vm_ref, qg_ref, kg_ref, vg_ref, gate_ref):
    x = x_ref[0]
    mod = mod_ref[0]
    h = (_rms(x) * g1_ref[...]) * (1.0 + mod[1:2]) + mod[0:1]
    hb = h.astype(BF16)

    def proj(lo, hi):
        return _dot(hb, win_ref[:, lo:hi])

    qcn = (_rms(proj(C_QC, C_KVC)) * gq_ref[...]).astype(BF16)
    q = _dot(qcn, wqb_ref[...])
    qs = _dot(qcn, wqbs_ref[...])
    am = _tile_lanes(tab_ref[T_AM], MLA_HEADS)
    bm = _tile_lanes(tab_ref[T_BM], MLA_HEADS)
    qm_ref[0] = (q * am + qs * bm).astype(BF16)

    kvn = (_rms(proj(C_KVC, C_KPE)) * gkv_ref[...]).astype(BF16)
    kpe = proj(C_KPE, C_KPES) * tab_ref[T_CK] + proj(C_KPES, C_QG) * tab_ref[T_SK]
    kn = _dot(kvn, wkk_ref[...])
    km_ref[0] = (kn + _tile_lanes(kpe, MLA_HEADS)).astype(BF16)
    vm_ref[0] = _with_ones(_dot(kvn, wkv_ref[...]).astype(BF16))

    bd = bd_ref[...]
    qg = proj(C_QG, C_QGS)
    qgs = proj(C_QGS, C_KG)
    rq = lax.rsqrt(_group_mean_sq(qg, bd) + NORM_EPS)
    aq = _tile_lanes(tab_ref[T_AQ], 4)
    bq = _tile_lanes(tab_ref[T_BQ], 4)
    qg_ref[0] = (rq * (qg * aq + qgs * bq)).astype(BF16)

    kg = proj(C_KG, C_KGS)
    kgs = proj(C_KGS, C_VG)
    rk = lax.rsqrt(_group_mean_sq(kg, bd[:256, :256]) + NORM_EPS)
    ak = _tile_lanes(tab_ref[T_AK], 2)
    bk = _tile_lanes(tab_ref[T_BK], 2)
    kg_ref[0] = (rk * (kg * ak + kgs * bk)).astype(BF16)
    vg_ref[0] = _with_ones(proj(C_VG, C_GATE).astype(BF16))

    gate_ref[0] = jax.nn.sigmoid(proj(C_GATE, C_END)).astype(BF16)


def _inproj_call(xs, mod, g1, lw, tabs, *, n_ctx, tm):
    b, t, d = xs.shape
    nlt = (t - n_ctx) // tm
    ctx_row = b

    def tok(i, bb):
        return (bb, i, 0)

    def modmap(i, bb):
        return (jnp.where(i >= nlt, ctx_row, bb), 0, 0)

    out_w = (1024, 1024, 1024, 512, 256, 512, 2048)
    return pl.pallas_call(
        _inproj_kernel,
        grid=(t // tm, b),
        in_specs=[
            pl.BlockSpec((1, tm, d), tok),
            pl.BlockSpec((1, 6, d), modmap),
            _const_spec((1, d)),
            _const_spec((d, C_END)),
            _const_spec((1, MLA_Q_LORA)),
            _const_spec((MLA_Q_LORA, 1024)),
            _const_spec((MLA_Q_LORA, 1024)),
            _const_spec((1, MLA_KV_LORA)),
            _const_spec((MLA_KV_LORA, 1024)),
            _const_spec((MLA_KV_LORA, 512)),
            _const_spec((512, 512)),
            pl.BlockSpec((8, tm, LANE), lambda i, bb: (0, i, 0)),
        ],
        out_specs=[pl.BlockSpec((1, tm, w), tok) for w in out_w],
        out_shape=[jax.ShapeDtypeStruct((b, t, w), BF16) for w in out_w],
        compiler_params=_cparams(("parallel", "arbitrary")),
        name="inproj",
    )(xs, mod, g1, lw["w_in"], lw["gq"], lw["wqb"], lw["wqbs"], lw["gkv"],
      lw["wkk"], lw["wkv"], lw["bd"], tabs)


def _fold_lanes(x, op):
    out = x[:, :LANE]
    for lo in range(LANE, x.shape[1], LANE):
        out = op(out, x[:, lo:lo + LANE])
    return out


def _key_block(nkeys):
    for kb in (768, 512, 256):
        if nkeys % kb == 0:
            return kb
    raise ValueError(nkeys)


def _mask_gqa_q(q):
    lane = lax.broadcasted_iota(jnp.int32, q.shape, 1)
    keep = (lane // 64) == (pl.program_id(1) % 2)
    return jnp.where(keep, q, jnp.zeros_like(q))


def _attn_stages(nkeys, k_ref, v_ref, scores=None, probs=None, values=None):
    kb = _key_block(nkeys)
    mrun = acc = None
    if probs is not None:
        m = jnp.max(probs[2][...], axis=-1, keepdims=True)
    for lo in range(0, nkeys, kb):
        if scores is not None:
            q, s_ref, _ = scores
            s = lax.dot_general(q, k_ref[0, lo:lo + kb, :], (((1,), (1,)), ((), ())),
                                preferred_element_type=F32)
            s_ref[:, lo:lo + kb] = s
            f = _fold_lanes(s, jnp.maximum)
            mrun = f if mrun is None else jnp.maximum(mrun, f)
        if probs is not None:
            s_ref, p_ref, _ = probs
            p_ref[:, lo:lo + kb] = jnp.exp2(s_ref[:, lo:lo + kb] - m).astype(BF16)
        if values is not None:
            part = _dot(values[0][:, lo:lo + kb], v_ref[0, lo:lo + kb, :])
            acc = part if acc is None else acc + part
    if scores is not None:
        scores[2][...] = mrun
    if values is not None:
        values[1]((acc[:, :LANE] / acc[:, LANE:LANE + 1]).astype(BF16))


def _attn_ctx_kernel(q_ref, k_ref, v_ref, o_ref, s_scr, p_scr, m_scr, *, gqa):
    q = q_ref[0]
    if gqa:
        q = _mask_gqa_q(q)
    nkeys = k_ref.shape[1]

    def store(o):
        o_ref[0] = o

    _attn_stages(nkeys, k_ref, v_ref, scores=(q, s_scr, m_scr))
    _attn_stages(nkeys, k_ref, v_ref, probs=(s_scr, p_scr, m_scr))
    _attn_stages(nkeys, k_ref, v_ref, values=(p_scr, store))


def _attn_lat_kernel(q_ref, k_ref, v_ref, oin_ref, o_ref, s0, s1, p0, p1, m0, m1,
                     *, gqa, tq, nkeys):
    del oin_ref
    j = pl.program_id(2)
    last = pl.num_programs(2) - 1
    s_scr, p_scr, m_scr = (s0, s1), (p0, p1), (m0, m1)

    def scores(slot):
        q = q_ref[0, slot * tq:(slot + 1) * tq, :]
        if gqa:
            q = _mask_gqa_q(q)
        return (q, s_scr[slot], m_scr[slot])

    def probs(slot):
        return (s_scr[slot], p_scr[slot], m_scr[slot])

    def values(slot):
        def store(o):
            o_ref[0, slot * tq:(slot + 1) * tq, :] = o
        return (p_scr[slot], store)

    stages = functools.partial(_attn_stages, nkeys, k_ref, v_ref)

    @pl.when(j == 0)
    def _():
        stages(scores=scores(0))
        stages(scores=scores(1), probs=probs(0))

    @pl.when((j > 0) & (j < last))
    def _():
        stages(scores=scores(0))
        stages(probs=probs(1))
        stages(values=values(0))
        stages(scores=scores(1))
        stages(probs=probs(0))
        stages(values=values(1))

    @pl.when(j == last)
    def _():
        stages(probs=probs(1), values=values(0))
        stages(values=values(1))


def _attn_call(q, k, v, *, gqa, n_ctx, tq):
    b, t, _ = q.shape
    n_lat = t - n_ctx
    heads = GQA_HEADS if gqa else MLA_HEADS
    if gqa:
        qcol = lambda h: h // 2
        kcol = lambda h: h // GQA_GROUP
        vcol = kcol
    else:
        qcol = lambda h: h
        kcol = lambda h: h
        vcol = lambda h: h // 2
    out_shape = jax.ShapeDtypeStruct((b, t, heads * LANE), BF16)
    sem = ("parallel", "parallel", "arbitrary")

    cblk = n_lat // n_ctx
    o_ctx = pl.pallas_call(
        functools.partial(_attn_ctx_kernel, gqa=gqa),
        grid=(b, heads),
        in_specs=[
            pl.BlockSpec((1, n_ctx, LANE), lambda bb, h: (bb, cblk, qcol(h))),
            pl.BlockSpec((1, n_ctx, LANE), lambda bb, h: (bb, cblk, kcol(h))),
            pl.BlockSpec((1, n_ctx, 2 * LANE), lambda bb, h: (bb, cblk, vcol(h))),
        ],
        out_specs=pl.BlockSpec((1, n_ctx, LANE), lambda bb, h: (bb, cblk, h)),
        out_shape=out_shape,
        scratch_shapes=[pltpu.VMEM((n_ctx, n_ctx), F32), pltpu.VMEM((n_ctx, n_ctx), BF16),
                        pltpu.VMEM((n_ctx, LANE), F32)],
        compiler_params=_cparams(sem[:2]),
        name="attn_ctx_gqa" if gqa else "attn_ctx_mla",
    )(q, k, v)

    npairs = n_lat // (2 * tq)
    kern = functools.partial(_attn_lat_kernel, gqa=gqa, tq=tq, nkeys=t)
    return pl.pallas_call(
        kern,
        grid=(b, heads, npairs + 1),
        in_specs=[
            pl.BlockSpec((1, 2 * tq, LANE),
                         lambda bb, h, j: (bb, jnp.minimum(j, npairs - 1), qcol(h))),
            pl.BlockSpec((1, t, LANE), lambda bb, h, j: (bb, 0, kcol(h))),
            pl.BlockSpec((1, t, 2 * LANE), lambda bb, h, j: (bb, 0, vcol(h))),
            pl.BlockSpec(memory_space=pl.ANY),
        ],
        out_specs=pl.BlockSpec((1, 2 * tq, LANE),
                               lambda bb, h, j: (bb, jnp.maximum(j - 1, 0), h)),
        out_shape=out_shape,
        scratch_shapes=[pltpu.VMEM((tq, t), F32)] * 2 + [pltpu.VMEM((tq, t), BF16)] * 2
        + [pltpu.VMEM((tq, LANE), F32)] * 2,
        input_output_aliases={3: 0},
        compiler_params=_cparams(sem, ATTN_FLAGS),
        name="attn_gqa" if gqa else "attn_mla",
    )(q, k, v, o_ctx)


def _top2_combine(logits):
    lane = lax.broadcasted_iota(jnp.int32, logits.shape, 1)
    neg = jnp.float32(-jnp.inf)
    lg = jnp.where(lane < N_EXPERTS, logits, neg)
    m1 = jnp.max(lg, axis=-1, keepdims=True)
    i1 = jnp.min(jnp.where(lg == m1, lane, LANE), axis=-1, keepdims=True)
    sel1 = lane == i1
    lg2 = jnp.where(sel1, neg, lg)
    m2 = jnp.max(lg2, axis=-1, keepdims=True)
    i2 = jnp.min(jnp.where(lg2 == m2, lane, LANE), axis=-1, keepdims=True)
    sel2 = lane == i2
    e2 = jnp.exp(m2 - m1)
    w1 = 1.0 / (1.0 + e2)
    w2 = e2 / (1.0 + e2)
    return jnp.where(sel1, w1, 0.0) + jnp.where(sel2, w2, 0.0)


def _post_kernel(*refs, moe):
    if moe:
        (om_ref, og_ref, gate_ref, x_ref, mod_ref, wom_ref, wog_ref, wout_ref, g2_ref,
         rhi_ref, rlo_ref, xo_ref, h2_ref, comb_ref) = refs
    else:
        (om_ref, og_ref, gate_ref, x_ref, mod_ref, wom_ref, wog_ref, wout_ref, g2_ref,
         xo_ref, h2_ref) = refs
    a = _dot(om_ref[0], wom_ref[...])
    bb = _dot(og_ref[0], wog_ref[...])
    gate = gate_ref[0]
    merged = gate[:, :1024].astype(F32) * a + gate[:, 1024:].astype(F32) * bb
    mix = _dot(merged.astype(BF16), wout_ref[...])
    mod = mod_ref[0]
    x1 = x_ref[0] + mod[2:3] * mix
    xo_ref[0] = x1
    h2 = (_rms(x1) * g2_ref[...]) * (1.0 + mod[4:5]) + mod[3:4]
    h2_ref[0] = h2.astype(BF16)
    if moe:
        h_hi = h2.astype(BF16)
        h_lo = (h2 - h_hi.astype(F32)).astype(BF16)
        logits = (_dot(h_hi, rhi_ref[...]) + _dot(h_lo, rhi_ref[...])
                  + _dot(h_hi, rlo_ref[...]))
        comb_ref[0] = _top2_combine(logits)


def _post_call(om, og, gates, xs, mod, lw, *, n_ctx, tm, moe):
    b, t, d = xs.shape
    nlt = (t - n_ctx) // tm
    ctx_row = b

    def tok(i, bb):
        return (bb, i, 0)

    def modmap(i, bb):
        return (jnp.where(i >= nlt, ctx_row, bb), 0, 0)

    in_specs = [
        pl.BlockSpec((1, tm, MLA_HEADS * LANE), tok),
        pl.BlockSpec((1, tm, GQA_HEADS * LANE), tok),
        pl.BlockSpec((1, tm, 2048), tok),
        pl.BlockSpec((1, tm, d), tok),
        pl.BlockSpec((1, 6, d), modmap),
        _const_spec((MLA_HEADS * LANE, d)),
        _const_spec((GQA_HEADS * LANE, d)),
        _const_spec((d, d)),
        _const_spec((1, d)),
    ]
    args = [om, og, gates, xs, mod, lw["wom"], lw["wog"], lw["wout"], lw["g2"]]
    out_specs = [pl.BlockSpec((1, tm, d), tok), pl.BlockSpec((1, tm, d), tok)]
    out_shape = [jax.ShapeDtypeStruct((b, t, d), F32), jax.ShapeDtypeStruct((b, t, d), BF16)]
    if moe:
        in_specs += [_const_spec((d, LANE)), _const_spec((d, LANE))]
        args += [lw["r_hi"], lw["r_lo"]]
        out_specs.append(pl.BlockSpec((1, tm, LANE), tok))
        out_shape.append(jax.ShapeDtypeStruct((b, t, LANE), F32))
    return pl.pallas_call(
        functools.partial(_post_kernel, moe=moe),
        grid=(t // tm, b),
        in_specs=in_specs,
        out_specs=out_specs,
        out_shape=out_shape,
        input_output_aliases={3: 0},
        compiler_params=_cparams(("parallel", "arbitrary")),
        name="post_moe" if moe else "post_dense",
    )(*args)


def _swiglu_chunk(hb, w1, w3, w2):
    a = _dot(hb, w1)
    g = (a * jax.nn.sigmoid(a)) * _dot(hb, w3)
    return _dot(g.astype(BF16), w2)


def _dense_ffn_kernel(h_ref, x_ref, mod_ref, w1_ref, w3_ref, w2_ref, o_ref, *, fc):
    hb = h_ref[0]
    f = w1_ref.shape[1]
    acc = None
    for lo in range(0, f, fc):
        part = _swiglu_chunk(hb, w1_ref[:, lo:lo + fc], w3_ref[:, lo:lo + fc],
                             w2_ref[lo:lo + fc, :])
        acc = part if acc is None else acc + part
    o_ref[0] = x_ref[0] + mod_ref[0][5:6] * acc


def _dense_ffn_call(h2, xs, mod, lw, *, n_ctx, tm):
    b, t, d = xs.shape
    f = lw["w1"].shape[1]
    nlt = (t - n_ctx) // tm
    ctx_row = b

    def tok(i, bb):
        return (bb, i, 0)

    def modmap(i, bb):
        return (jnp.where(i >= nlt, ctx_row, bb), 0, 0)

    return pl.pallas_call(
        functools.partial(_dense_ffn_kernel, fc=f // 2),
        grid=(t // tm, b),
        in_specs=[
            pl.BlockSpec((1, tm, d), tok),
            pl.BlockSpec((1, tm, d), tok),
            pl.BlockSpec((1, 6, d), modmap),
            pl.BlockSpec((d, f), lambda i, bb: (0, 0), pipeline_mode=pl.Buffered(1)),
            pl.BlockSpec((d, f), lambda i, bb: (0, 0), pipeline_mode=pl.Buffered(1)),
            pl.BlockSpec((f, d), lambda i, bb: (0, 0), pipeline_mode=pl.Buffered(1)),
        ],
        out_specs=pl.BlockSpec((1, tm, d), tok),
        out_shape=jax.ShapeDtypeStruct((b, t, d), F32),
        input_output_aliases={1: 0},
        compiler_params=_cparams(("parallel", "arbitrary")),
        name="ffn_dense",
    )(h2, xs, mod, lw["w1"], lw["w3"], lw["w2"])


MOE_F_SPLIT = 2


def _moe_tile(n):
    for tb in (1024, 768, 512, 256):
        if n % tb == 0:
            return tb
    raise ValueError(n)


def _moe_rows(tb):
    share = tb * 2 // N_EXPERTS
    return -(-(share + share // 8) // 16) * 16


def _moe_ffn_kernel(h_ref, comb_ref, tri_ref, w1_ref, w3_ref, w2_ref, o_ref,
                    sel_scr, pos_scr, xg_scr, y_scr, *, rows):
    e = pl.program_id(1)
    fh = pl.program_id(2)
    tb = h_ref.shape[0]

    @pl.when((e == 0) & (fh == 0))
    def _():
        sel_t = (comb_ref[...].T[:16] > 0.0).astype(BF16)
        sel_scr[...] = sel_t.astype(F32)
        pos_scr[...] = lax.dot_general(sel_t, tri_ref[...], (((1,), (1,)), ((), ())),
                                       preferred_element_type=F32)
        o_ref[...] = jnp.zeros_like(o_ref)

    sel_e = sel_scr[pl.ds(e, 1), :]
    rank_e = jnp.where(sel_e > 0.0, pos_scr[pl.ds(e, 1), :], -1.0)
    count = jnp.sum(sel_e).astype(jnp.int32)
    nsub = (count + rows - 1) // rows

    def one_hot(s):
        j = lax.broadcasted_iota(jnp.int32, (rows, tb), 0) + s * rows
        return jnp.where(rank_e == j.astype(F32), 1.0, 0.0).astype(BF16)

    def block(s):
        return pl.ds(pl.multiple_of(s * rows, 16), rows)

    @pl.when(fh == 0)
    def _():
        def gather(s, carry):
            xg_scr[block(s), :] = _dot(one_hot(s), h_ref[...]).astype(BF16)
            return carry
        lax.fori_loop(0, nsub, gather, 0)

    def expert(s, carry):
        part = _swiglu_chunk(xg_scr[block(s), :], w1_ref[0], w3_ref[0], w2_ref[0])

        @pl.when(fh == 0)
        def _():
            y_scr[block(s), :] = part

        @pl.when(fh != 0)
        def _():
            y_scr[block(s), :] = y_scr[block(s), :] + part
        return carry
    lax.fori_loop(0, nsub, expert, 0)

    @pl.when(fh == pl.num_programs(2) - 1)
    def _():
        comb = comb_ref[...]
        lane = lax.broadcasted_iota(jnp.int32, comb.shape, 1)
        w_col = jnp.sum(jnp.where(lane == e, comb, 0.0), axis=-1, keepdims=True)

        def scatter(s, carry):
            z = lax.dot_general(one_hot(s), y_scr[block(s), :].astype(BF16),
                                (((0,), (0,)), ((), ())), preferred_element_type=F32)
            o_ref[...] += w_col * z
            return carry
        lax.fori_loop(0, nsub, scatter, 0)


def _moe_ffn_call(h2, comb, lw):
    n, d = h2.shape
    ne, _, f = lw["w1"].shape
    fc = f // MOE_F_SPLIT
    tb = _moe_tile(n)
    rows = _moe_rows(tb)
    nsub_max = -(-tb // rows)
    tri = (lax.broadcasted_iota(jnp.int32, (tb, tb), 1)
           < lax.broadcasted_iota(jnp.int32, (tb, tb), 0)).astype(BF16)
    return pl.pallas_call(
        functools.partial(_moe_ffn_kernel, rows=rows),
        grid=(n // tb, ne, MOE_F_SPLIT),
        in_specs=[
            pl.BlockSpec((tb, d), lambda i, e, fh: (i, 0)),
            pl.BlockSpec((tb, LANE), lambda i, e, fh: (i, 0)),
            pl.BlockSpec((tb, tb), lambda i, e, fh: (0, 0), pipeline_mode=pl.Buffered(1)),
            pl.BlockSpec((1, d, fc), lambda i, e, fh: (e, 0, fh)),
            pl.BlockSpec((1, d, fc), lambda i, e, fh: (e, 0, fh)),
            pl.BlockSpec((1, fc, d), lambda i, e, fh: (e, fh, 0)),
        ],
        out_specs=pl.BlockSpec((tb, d), lambda i, e, fh: (i, 0)),
        out_shape=jax.ShapeDtypeStruct((n, d), F32),
        scratch_shapes=[pltpu.VMEM((16, tb), F32), pltpu.VMEM((16, tb), F32),
                        pltpu.VMEM((nsub_max * rows, d), BF16),
                        pltpu.VMEM((nsub_max * rows, d), F32)],
        compiler_params=_cparams(("parallel", "arbitrary", "arbitrary")),
        name="ffn_moe",
    )(h2, comb, tri, lw["w1"], lw["w3"], lw["w2"])


def _residual_kernel(x_ref, f_ref, mod_ref, o_ref):
    o_ref[0] = x_ref[0] + mod_ref[0][5:6] * f_ref[0]


def _residual_call(xs, f, mod, *, n_ctx, tm):
    b, t, d = xs.shape
    nlt = (t - n_ctx) // tm
    ctx_row = b

    def tok(i, bb):
        return (bb, i, 0)

    def modmap(i, bb):
        return (jnp.where(i >= nlt, ctx_row, bb), 0, 0)

    return pl.pallas_call(
        _residual_kernel,
        grid=(t // tm, b),
        in_specs=[pl.BlockSpec((1, tm, d), tok), pl.BlockSpec((1, tm, d), tok),
                  pl.BlockSpec((1, 6, d), modmap)],
        out_specs=pl.BlockSpec((1, tm, d), tok),
        out_shape=jax.ShapeDtypeStruct((b, t, d), F32),
        input_output_aliases={0: 0},
        compiler_params=_cparams(("parallel", "arbitrary")),
        name="moe_residual",
    )(xs, f, mod)


def _final_norm_kernel(x_ref, g_ref, o_ref):
    o_ref[0] = _rms(x_ref[0]) * g_ref[...]


def _final_norm_call(xs, g, *, n_ctx, tm):
    b, t, d = xs.shape
    n_lat = t - n_ctx
    return pl.pallas_call(
        _final_norm_kernel,
        grid=(n_lat // tm, b),
        in_specs=[pl.BlockSpec((1, tm, d), lambda i, bb: (bb, i, 0)),
                  _const_spec((1, d))],
        out_specs=pl.BlockSpec((1, tm, d), lambda i, bb: (bb, i, 0)),
        out_shape=jax.ShapeDtypeStruct((b, n_lat, d), F32),
        compiler_params=_cparams(("parallel", "arbitrary")),
        name="final_norm",
    )(xs, g)


def _rope_parts(n_ctx, n_lat, dim):
    half = dim // 2
    quarter = half // 2
    inv = ROPE_THETA ** (-(jnp.arange(quarter, dtype=F32) * 2.0) / half)
    tpos = jnp.arange(n_lat)
    row = (tpos // GRID_W).astype(F32)
    col = (tpos % GRID_W).astype(F32)
    zeros = jnp.zeros((n_ctx, quarter), F32)
    ang_r = jnp.concatenate([row[:, None] * inv, zeros], axis=0)
    ang_c = jnp.concatenate([col[:, None] * inv, zeros], axis=0)
    cr, sr, cc, sc = jnp.cos(ang_r), jnp.sin(ang_r), jnp.cos(ang_c), jnp.sin(ang_c)
    cos = jnp.concatenate([cr, cr, cc, cc], axis=1)
    sin = jnp.concatenate([-sr, sr, -sc, sc], axis=1)
    return cos, sin


def _swap_perm(dim):
    q = dim // 4
    idx = np.arange(dim)
    return np.concatenate([idx[q:2 * q], idx[:q], idx[3 * q:], idx[2 * q:3 * q]])


def _layer_tables(cos_m, sin_m, cos_g, sin_g, gq, gk):
    t = cos_m.shape[0]
    z64 = jnp.zeros((t, 64), F32)
    z32 = jnp.zeros((t, 32), F32)
    sm = MLA_SCALE * LOG2E
    sg = GQA_SCALE * LOG2E
    pg = _swap_perm(GQA_HEAD_DIM)
    am = jnp.concatenate([jnp.full((t, 64), sm, F32), sm * cos_m, z32], axis=1)
    bm = jnp.concatenate([z64, sm * sin_m, z32], axis=1)
    ck = jnp.concatenate([z64, cos_m, z32], axis=1)
    sk = jnp.concatenate([z64, sin_m, z32], axis=1)
    aq = sg * cos_g * gq[None, :]
    bq = sg * sin_g * gq[pg][None, :]
    ak = cos_g * gk[None, :]
    bk = sin_g * gk[pg][None, :]
    two = lambda a: jnp.concatenate([a, a], axis=1)
    return jnp.stack([am, bm, ck, sk, two(aq), two(bq), two(ak), two(bk)], axis=0)


def _layer_weights(l, w_in, mla_w_qb, mla_w_kvb, w_o_mla, w_o_gqa, w_out):
    d = w_in.shape[1]
    wl = w_in[l]
    offs = np.cumsum([0, MLA_Q_LORA, MLA_KV_LORA, MLA_ROPE, 512, 128, 128, d, d])
    qc, kvc, kpe, qg, kg, vg, gm, gg = [wl[:, offs[i]:offs[i + 1]] for i in range(8)]
    pm = _swap_perm(MLA_ROPE)
    pg = _swap_perm(GQA_HEAD_DIM)
    pg8 = np.concatenate([pg + 64 * h for h in range(GQA_HEADS)])
    pg2 = np.concatenate([pg + 64 * h for h in range(GQA_KV_HEADS)])

    def pad_kpe(w):
        return jnp.pad(w, ((0, 0), (64, 32)))

    def dup(w):
        return jnp.concatenate([w[:, :64], w[:, :64], w[:, 64:], w[:, 64:]], axis=1)

    w_ext = jnp.concatenate(
        [qc, kvc, pad_kpe(kpe), pad_kpe(kpe[:, pm]), qg, qg[:, pg8],
         dup(kg), dup(kg[:, pg2]), dup(vg), gm, gg], axis=1).astype(BF16)

    wqb = mla_w_qb[l].reshape(MLA_Q_LORA, MLA_HEADS, MLA_NOPE + MLA_ROPE)
    rope_sw = wqb[:, :, MLA_NOPE:][:, :, pm]
    wqb_pad = jnp.pad(wqb, ((0, 0), (0, 0), (0, 32))).reshape(MLA_Q_LORA, 1024)
    wqbs_pad = jnp.pad(rope_sw, ((0, 0), (0, 0), (64, 32))).reshape(MLA_Q_LORA, 1024)

    wkvb = mla_w_kvb[l].reshape(MLA_KV_LORA, MLA_HEADS, MLA_NOPE + MLA_V)
    wkk = jnp.pad(wkvb[:, :, :MLA_NOPE], ((0, 0), (0, 0), (0, 64))).reshape(MLA_KV_LORA, 1024)
    wkv = wkvb[:, :, MLA_NOPE:].reshape(MLA_KV_LORA, MLA_HEADS * MLA_V)

    zero = jnp.zeros((MLA_HEADS // 2, 64, d), F32)
    wom = w_o_mla[l].reshape(MLA_HEADS // 2, 2, 64, d)
    wom = jnp.stack([wom[:, 0], zero, zero, wom[:, 1]], axis=1).reshape(MLA_HEADS * LANE, d)
    wog = jnp.pad(w_o_gqa[l].reshape(GQA_HEADS, 64, d), ((0, 0), (0, 64), (0, 0)))
    wog = wog.reshape(GQA_HEADS * LANE, d)

    eye = np.kron(np.eye(8, dtype=np.float32), np.full((64, 64), 1.0 / 64, np.float32))
    return {
        "w_in": w_ext,
        "wqb": wqb_pad.astype(BF16),
        "wqbs": wqbs_pad.astype(BF16),
        "wkk": wkk.astype(BF16),
        "wkv": wkv.astype(BF16),
        "bd": jnp.asarray(eye, BF16),
        "wom": wom.astype(BF16),
        "wog": wog.astype(BF16),
        "wout": w_out[l].astype(BF16),
    }


def kernel(x, c, ctx, c_ctx, w_ada, b_ada, norm1_g, norm2_g, w_in, mla_q_norm_g, mla_w_qb,
           mla_kv_norm_g, mla_w_kvb, gqa_q_norm_g, gqa_k_norm_g, w_o_mla, w_o_gqa, w_out,
           dense_w1, dense_w3, dense_w2, moe_router, moe_w1, moe_w3, moe_w2, final_norm_g):
    b, n_lat, d = x.shape
    n_ctx = ctx.shape[1]
    depth = w_ada.shape[0]
    t = n_ctx + n_lat
    tm = 256
    assert n_ctx % tm == 0 and n_lat % tm == 0 and n_lat % GRID_W == 0

    rows = -(-(b + 1) // 8) * 8
    cc = jnp.zeros((rows, d), F32).at[:b].set(c).at[b].set(c_ctx)
    mod_all = _ada_call(cc, w_ada, b_ada).reshape(depth, rows, 6, d)

    cos_m, sin_m = _rope_parts(n_ctx, n_lat, MLA_ROPE)
    cos_g, sin_g = _rope_parts(n_ctx, n_lat, GQA_HEAD_DIM)

    xs = jnp.concatenate([x, ctx], axis=1)
    for l in range(depth):
        lw = _layer_weights(l, w_in, mla_w_qb, mla_w_kvb, w_o_mla, w_o_gqa, w_out)
        lw["gq"] = mla_q_norm_g[l][None, :]
        lw["gkv"] = mla_kv_norm_g[l][None, :]
        lw["g2"] = norm2_g[l][None, :]
        tabs = _layer_tables(cos_m, sin_m, cos_g, sin_g, gqa_q_norm_g[l], gqa_k_norm_g[l])
        mod = mod_all[l]
        moe = l % 2 == 1

        qm, km, vm, qg, kg, vg, gates = _inproj_call(
            xs, mod, norm1_g[l][None, :], lw, tabs, n_ctx=n_ctx, tm=tm)
        om = _attn_call(qm, km, vm, gqa=False, n_ctx=n_ctx, tq=256)
        og = _attn_call(qg, kg, vg, gqa=True, n_ctx=n_ctx, tq=256)

        if moe:
            r = jnp.pad(moe_router[l // 2], ((0, 0), (0, LANE - N_EXPERTS)))
            lw["r_hi"] = r.astype(BF16)
            lw["r_lo"] = (r - lw["r_hi"].astype(F32)).astype(BF16)
            xs, h2, comb = _post_call(om, og, gates, xs, mod, lw, n_ctx=n_ctx, tm=tm, moe=True)
            lw["w1"] = moe_w1[l // 2].astype(BF16)
            lw["w3"] = moe_w3[l // 2].astype(BF16)
            lw["w2"] = moe_w2[l // 2].astype(BF16)
            f = _moe_ffn_call(h2.reshape(b * t, d), comb.reshape(b * t, LANE), lw)
            xs = _residual_call(xs, f.reshape(b, t, d), mod, n_ctx=n_ctx, tm=tm)
        else:
            xs, h2 = _post_call(om, og, gates, xs, mod, lw, n_ctx=n_ctx, tm=tm, moe=False)
            lw["w1"] = dense_w1[l // 2].astype(BF16)
            lw["w3"] = dense_w3[l // 2].astype(BF16)
            lw["w2"] = dense_w2[l // 2].astype(BF16)
            xs = _dense_ffn_call(h2, xs, mod, lw, n_ctx=n_ctx, tm=tm)

    return _final_norm_call(xs, final_norm_g[None, :], n_ctx=n_ctx, tm=tm)
```

```python
import functools
import math

import numpy as np
import jax
import jax.numpy as jnp
from jax import lax
from jax.experimental import pallas as pl
from jax.experimental.pallas import tpu as pltpu

F32 = jnp.float32
BF16 = jnp.bfloat16

GRID_W = 64
ROPE_THETA = 10000.0
NORM_EPS = 1e-6
MLA_HEADS = 8
MLA_Q_LORA = 384
MLA_KV_LORA = 256
MLA_NOPE = 64
MLA_ROPE = 32
MLA_V = 64
MLA_SCALE = (MLA_NOPE + MLA_ROPE) ** -0.5
GQA_HEADS = 8
GQA_KV_HEADS = 2
GQA_GROUP = GQA_HEADS // GQA_KV_HEADS
GQA_HEAD_DIM = 64
GQA_SCALE = GQA_HEAD_DIM ** -0.5
N_EXPERTS = 8
LOG2E = math.log2(math.e)

LANE = 128
VMEM_LIMIT = 56 * 1024 * 1024

C_QC = 0
C_KVC = C_QC + MLA_Q_LORA
C_KPE = C_KVC + MLA_KV_LORA
C_KPES = C_KPE + LANE
C_QG = C_KPES + LANE
C_QGS = C_QG + 512
C_KG = C_QGS + 512
C_KGS = C_KG + 256
C_GATE = C_KGS + 256
C_END = C_GATE + 2048

T_AM, T_BM, T_CK, T_SK, T_AQ, T_BQ, T_AK, T_BK = range(8)


def _cparams(sem, flags=None):
    return pltpu.CompilerParams(dimension_semantics=sem, vmem_limit_bytes=VMEM_LIMIT,
                                flags=flags)


ATTN_FLAGS = None


def _const_spec(shape):
    nd = len(shape)
    return pl.BlockSpec(shape, lambda *_: (0,) * nd)


def _dot(a, b):
    return jnp.dot(a, b, preferred_element_type=F32)


def _rms(x):
    return x * lax.rsqrt(jnp.mean(x * x, axis=-1, keepdims=True) + NORM_EPS)


def _ada_kernel(cc_ref, w_ref, b_ref, o_ref):
    a = cc_ref[...]
    a = a * jax.nn.sigmoid(a)
    a_hi = a.astype(BF16)
    a_lo = (a - a_hi.astype(F32)).astype(BF16)
    w = w_ref[0]
    w_hi = w.astype(BF16)
    w_lo = (w - w_hi.astype(F32)).astype(BF16)
    acc = _dot(a_hi, w_hi) + _dot(a_lo, w_hi) + _dot(a_hi, w_lo)
    o_ref[0] = acc + b_ref[0]


def _ada_call(cc, w_ada, b_ada):
    depth, d, n = w_ada.shape
    rows = cc.shape[0]
    tn = 1536
    return pl.pallas_call(
        _ada_kernel,
        grid=(depth, n // tn),
        in_specs=[
            pl.BlockSpec((rows, d), lambda l, j: (0, 0)),
            pl.BlockSpec((1, d, tn), lambda l, j: (l, 0, j)),
            pl.BlockSpec((1, 1, tn), lambda l, j: (l, 0, j)),
        ],
        out_specs=pl.BlockSpec((1, rows, tn), lambda l, j: (l, 0, j)),
        out_shape=jax.ShapeDtypeStruct((depth, rows, n), F32),
        compiler_params=_cparams(("arbitrary", "arbitrary")),
        name="ada",
    )(cc, w_ada, b_ada.reshape(depth, 1, n))


def _group_mean_sq(x, bd):
    x2 = x * x
    hi = x2.astype(BF16)
    lo = (x2 - hi.astype(F32)).astype(BF16)
    return _dot(hi, bd) + _dot(lo, bd)


def _tile_lanes(t, n):
    return jnp.concatenate([t] * n, axis=1)


def _dot_nt(a, b):
    return lax.dot_general(a, b, (((1,), (1,)), ((), ())), preferred_element_type=F32)


def _inproj_kernel(x_ref, mod_ref, g1_ref, win_ref, gq_ref, wqb_ref, wqbs_ref, gkv_ref,
                   wkk_ref, wkvt_ref, wvgt_ref, bd_ref, tab_ref,
                   qm_ref, km_ref, vm_ref, qg_ref, kg_ref, vg_ref, gate_ref):
    x = x_ref[0]
    mod = mod_ref[0]
    h = (_rms(x) * g1_ref[...]) * (1.0 + mod[1:2]) + mod[0:1]
    hb = h.astype(BF16)

    def proj(lo, hi):
        return _dot(hb, win_ref[:, lo:hi])

    qcn = (_rms(proj(C_QC, C_KVC)) * gq_ref[...]).astype(BF16)
    q = _dot(qcn, wqb_ref[...])
    qs = _dot(qcn, wqbs_ref[...])
    am = _tile_lanes(tab_ref[T_AM], MLA_HEADS)
    bm = _tile_lanes(tab_ref[T_BM], MLA_HEADS)
    qm_ref[0] = (q * am + qs * bm).astype(BF16)

    kvn = (_rms(proj(C_KVC, C_KPE)) * gkv_ref[...]).astype(BF16)
    kpe = proj(C_KPE, C_KPES) * tab_ref[T_CK] + proj(C_KPES, C_QG) * tab_ref[T_SK]
    kn = _dot(kvn, wkk_ref[...])
    km_ref[0] = (kn + _tile_lanes(kpe, MLA_HEADS)).astype(BF16)
    tm = x.shape[0]
    vm_ref[0] = _dot_nt(wkvt_ref[...], kvn).astype(BF16).reshape(MLA_HEADS, MLA_V, tm)

    bd = bd_ref[...]
    qg = proj(C_QG, C_QGS)
    qgs = proj(C_QGS, C_KG)
    rq = lax.rsqrt(_group_mean_sq(qg, bd) + NORM_EPS)
    aq = _tile_lanes(tab_ref[T_AQ], 4)
    bq = _tile_lanes(tab_ref[T_BQ], 4)
    qg_ref[0] = (rq * (qg * aq + qgs * bq)).astype(BF16)

    kg = proj(C_KG, C_KGS)
    kgs = proj(C_KGS, C_GATE)
    rk = lax.rsqrt(_group_mean_sq(kg, bd[:256, :256]) + NORM_EPS)
    ak = _tile_lanes(tab_ref[T_AK], 2)
    bk = _tile_lanes(tab_ref[T_BK], 2)
    kg_ref[0] = (rk * (kg * ak + kgs * bk)).astype(BF16)
    vg_ref[0] = _dot_nt(wvgt_ref[...], hb).astype(BF16).reshape(
        GQA_KV_HEADS, GQA_HEAD_DIM, tm)

    gate_ref[0] = jax.nn.sigmoid(proj(C_GATE, C_END)).astype(BF16)


def _inproj_call(xs, mod, g1, lw, tabs, *, n_ctx, tm):
    b, t, d = xs.shape
    nlt = (t - n_ctx) // tm
    ctx_row = b

    def tok(i, bb):
        return (bb, i, 0)

    def modmap(i, bb):
        return (jnp.where(i >= nlt, ctx_row, bb), 0, 0)

    def tok_spec(w):
        return pl.BlockSpec((1, tm, w), tok), jax.ShapeDtypeStruct((b, t, w), BF16)

    def vt_spec(heads):
        return (pl.BlockSpec((1, heads, 64, tm), lambda i, bb: (bb, 0, 0, i)),
                jax.ShapeDtypeStruct((b, heads, 64, t), BF16))

    outs = [tok_spec(1024), tok_spec(1024), vt_spec(MLA_HEADS),
            tok_spec(512), tok_spec(256), vt_spec(GQA_KV_HEADS), tok_spec(2048)]
    return pl.pallas_call(
        _inproj_kernel,
        grid=(t // tm, b),
        in_specs=[
            pl.BlockSpec((1, tm, d), tok),
            pl.BlockSpec((1, 6, d), modmap),
            _const_spec((1, d)),
            _const_spec((d, C_END)),
            _const_spec((1, MLA_Q_LORA)),
            _const_spec((MLA_Q_LORA, 1024)),
            _const_spec((MLA_Q_LORA, 1024)),
            _const_spec((1, MLA_KV_LORA)),
            _const_spec((MLA_KV_LORA, 1024)),
            _const_spec((MLA_HEADS * MLA_V, MLA_KV_LORA)),
            _const_spec((GQA_KV_HEADS * GQA_HEAD_DIM, d)),
            _const_spec((512, 512)),
            pl.BlockSpec((8, tm, LANE), lambda i, bb: (0, i, 0)),
        ],
        out_specs=[o[0] for o in outs],
        out_shape=[o[1] for o in outs],
        compiler_params=_cparams(("parallel", "arbitrary")),
        name="inproj",
    )(xs, mod, g1, lw["w_in"], lw["gq"], lw["wqb"], lw["wqbs"], lw["gkv"],
      lw["wkk"], lw["wkvt"], lw["wvgt"], lw["bd"], tabs)


def _fold_lanes(x, op):
    out = x[:, :LANE]
    for lo in range(LANE, x.shape[1], LANE):
        out = op(out, x[:, lo:lo + LANE])
    return out


def _key_block(nkeys):
    for kb in (768, 512, 256):
        if nkeys % kb == 0:
            return kb
    raise ValueError(nkeys)


def _mask_gqa_q(q):
    lane = lax.broadcasted_iota(jnp.int32, q.shape, 1)
    keep = (lane // 64) == (pl.program_id(1) % 2)
    return jnp.where(keep, q, jnp.zeros_like(q))


def _attn_scores(q, k_ref, s_ref, m_ref, nkeys):
    kb = _key_block(nkeys)
    mrun = None
    for lo in range(0, nkeys, kb):
        s = lax.dot_general(k_ref[0, lo:lo + kb, :], q, (((1,), (1,)), ((), ())),
                            preferred_element_type=F32)
        s_ref[lo:lo + kb, :] = s
        f = jnp.max(s, axis=0, keepdims=True)
        mrun = f if mrun is None else jnp.maximum(mrun, f)
    m_ref[...] = jnp.broadcast_to(mrun, m_ref.shape)


def _attn_probs(s_ref, p_ref, m_ref, l_ref, nkeys):
    kb = _key_block(nkeys)
    m = m_ref[0:1, :]
    lrun = None
    for lo in range(0, nkeys, kb):
        p = jnp.exp2(s_ref[lo:lo + kb, :] - m)
        f = jnp.sum(p, axis=0, keepdims=True)
        lrun = f if lrun is None else lrun + f
        p_ref[lo:lo + kb, :] = p.astype(BF16)
    l_ref[...] = jnp.broadcast_to(lrun, l_ref.shape)


def _attn_values(p_ref, vt_ref, l_ref, nkeys):
    kb = _key_block(nkeys)
    acc = None
    for lo in range(0, nkeys, kb):
        part = _dot(vt_ref[0, 0, :, lo:lo + kb], p_ref[lo:lo + kb, :])
        acc = part if acc is None else acc + part
    o_t = acc / l_ref[0:1, :]
    o_t = jnp.concatenate([o_t, jnp.zeros_like(o_t)], axis=0)
    return o_t.T.astype(BF16)


def _attn_ctx_kernel(q_ref, k_ref, vt_ref, o_ref, s_scr, p_scr, m_scr, l_scr, *, gqa):
    q = q_ref[0]
    if gqa:
        q = _mask_gqa_q(q)
    nkeys = k_ref.shape[1]
    _attn_scores(q, k_ref, s_scr, m_scr, nkeys)
    _attn_probs(s_scr, p_scr, m_scr, l_scr, nkeys)
    o_ref[0] = _attn_values(p_scr, vt_ref, l_scr, nkeys)


def _attn_lat_kernel(q_ref, k_ref, vt_ref, oin_ref, o_ref,
                     s0, s1, p0, p1, m0, m1, l0, l1, *, gqa, tq, nkeys):
    del oin_ref
    j = pl.program_id(2)
    last = pl.num_programs(2) - 1
    s_scr, p_scr, m_scr, l_scr = (s0, s1), (p0, p1), (m0, m1), (l0, l1)

    def scores(slot):
        q = q_ref[0, slot * tq:(slot + 1) * tq, :]
        if gqa:
            q = _mask_gqa_q(q)
        _attn_scores(q, k_ref, s_scr[slot], m_scr[slot], nkeys)

    def probs(slot):
        _attn_probs(s_scr[slot], p_scr[slot], m_scr[slot], l_scr[slot], nkeys)

    def values(slot):
        o_ref[0, slot * tq:(slot + 1) * tq, :] = _attn_values(
            p_scr[slot], vt_ref, l_scr[slot], nkeys)

    @pl.when(j == 0)
    def _():
        scores(0)
        scores(1)
        probs(0)

    @pl.when((j > 0) & (j < last))
    def _():
        scores(0)
        probs(1)
        values(0)
        scores(1)
        probs(0)
        values(1)

    @pl.when(j == last)
    def _():
        probs(1)
        values(0)
        values(1)


def _attn_call(q, k, vt, *, gqa, n_ctx, tq):
    b, t, _ = q.shape
    n_lat = t - n_ctx
    heads = GQA_HEADS if gqa else MLA_HEADS
    if gqa:
        qcol = lambda h: h // 2
        kcol = lambda h: h // GQA_GROUP
        vhead = kcol
    else:
        qcol = lambda h: h
        kcol = lambda h: h
        vhead = lambda h: h
    out_shape = jax.ShapeDtypeStruct((b, t, heads * LANE), BF16)
    sem = ("parallel", "parallel", "arbitrary")

    cblk = n_lat // n_ctx
    o_ctx = pl.pallas_call(
        functools.partial(_attn_ctx_kernel, gqa=gqa),
        grid=(b, heads),
        in_specs=[
            pl.BlockSpec((1, n_ctx, LANE), lambda bb, h: (bb, cblk, qcol(h))),
            pl.BlockSpec((1, n_ctx, LANE), lambda bb, h: (bb, cblk, kcol(h))),
            pl.BlockSpec((1, 1, 64, n_ctx), lambda bb, h: (bb, vhead(h), 0, cblk)),
        ],
        out_specs=pl.BlockSpec((1, n_ctx, LANE), lambda bb, h: (bb, cblk, h)),
        out_shape=out_shape,
        scratch_shapes=[pltpu.VMEM((n_ctx, n_ctx), F32), pltpu.VMEM((n_ctx, n_ctx), BF16),
                        pltpu.VMEM((8, n_ctx), F32), pltpu.VMEM((8, n_ctx), F32)],
        compiler_params=_cparams(sem[:2]),
        name="attn_ctx_gqa" if gqa else "attn_ctx_mla",
    )(q, k, vt)

    npairs = n_lat // (2 * tq)
    kern = functools.partial(_attn_lat_kernel, gqa=gqa, tq=tq, nkeys=t)
    return pl.pallas_call(
        kern,
        grid=(b, heads, npairs + 1),
        in_specs=[
            pl.BlockSpec((1, 2 * tq, LANE),
                         lambda bb, h, j: (bb, jnp.minimum(j, npairs - 1), qcol(h))),
            pl.BlockSpec((1, t, LANE), lambda bb, h, j: (bb, 0, kcol(h))),
            pl.BlockSpec((1, 1, 64, t), lambda bb, h, j: (bb, vhead(h), 0, 0)),
            pl.BlockSpec(memory_space=pl.ANY),
        ],
        out_specs=pl.BlockSpec((1, 2 * tq, LANE),
                               lambda bb, h, j: (bb, jnp.maximum(j - 1, 0), h)),
        out_shape=out_shape,
        scratch_shapes=[pltpu.VMEM((t, tq), F32)] * 2 + [pltpu.VMEM((t, tq), BF16)] * 2
        + [pltpu.VMEM((8, tq), F32)] * 4,
        input_output_aliases={3: 0},
        compiler_params=_cparams(sem, ATTN_FLAGS),
        name="attn_gqa" if gqa else "attn_mla",
    )(q, k, vt, o_ctx)


def _top2_combine(logits):
    lane = lax.broadcasted_iota(jnp.int32, logits.shape, 1)
    neg = jnp.float32(-jnp.inf)
    lg = jnp.where(lane < N_EXPERTS, logits, neg)
    m1 = jnp.max(lg, axis=-1, keepdims=True)
    i1 = jnp.min(jnp.where(lg == m1, lane, LANE), axis=-1, keepdims=True)
    sel1 = lane == i1
    lg2 = jnp.where(sel1, neg, lg)
    m2 = jnp.max(lg2, axis=-1, keepdims=True)
    i2 = jnp.min(jnp.where(lg2 == m2, lane, LANE), axis=-1, keepdims=True)
    sel2 = lane == i2
    e2 = jnp.exp(m2 - m1)
    w1 = 1.0 / (1.0 + e2)
    w2 = e2 / (1.0 + e2)
    return jnp.where(sel1, w1, 0.0) + jnp.where(sel2, w2, 0.0)


def _post_kernel(*refs, moe):
    if moe:
        (om_ref, og_ref, gate_ref, x_ref, mod_ref, wom_ref, wog_ref, wout_ref, g2_ref,
         rhi_ref, rlo_ref, xo_ref, h2_ref, comb_ref) = refs
    else:
        (om_ref, og_ref, gate_ref, x_ref, mod_ref, wom_ref, wog_ref, wout_ref, g2_ref,
         xo_ref, h2_ref) = refs
    a = _dot(om_ref[0], wom_ref[...])
    bb = _dot(og_ref[0], wog_ref[...])
    gate = gate_ref[0]
    merged = gate[:, :1024].astype(F32) * a + gate[:, 1024:].astype(F32) * bb
    mix = _dot(merged.astype(BF16), wout_ref[...])
    mod = mod_ref[0]
    x1 = x_ref[0] + mod[2:3] * mix
    xo_ref[0] = x1
    h2 = (_rms(x1) * g2_ref[...]) * (1.0 + mod[4:5]) + mod[3:4]
    h2_ref[0] = h2.astype(BF16)
    if moe:
        h_hi = h2.astype(BF16)
        h_lo = (h2 - h_hi.astype(F32)).astype(BF16)
        logits = (_dot(h_hi, rhi_ref[...]) + _dot(h_lo, rhi_ref[...])
                  + _dot(h_hi, rlo_ref[...]))
        comb_ref[0] = _top2_combine(logits)


def _post_call(om, og, gates, xs, mod, lw, *, n_ctx, tm, moe):
    b, t, d = xs.shape
    nlt = (t - n_ctx) // tm
    ctx_row = b

    def tok(i, bb):
        return (bb, i, 0)

    def modmap(i, bb):
        return (jnp.where(i >= nlt, ctx_row, bb), 0, 0)

    in_specs = [
        pl.BlockSpec((1, tm, MLA_HEADS * LANE), tok),
        pl.BlockSpec((1, tm, GQA_HEADS * LANE), tok),
        pl.BlockSpec((1, tm, 2048), tok),
        pl.BlockSpec((1, tm, d), tok),
        pl.BlockSpec((1, 6, d), modmap),
        _const_spec((MLA_HEADS * LANE, d)),
        _const_spec((GQA_HEADS * LANE, d)),
        _const_spec((d, d)),
        _const_spec((1, d)),
    ]
    args = [om, og, gates, xs, mod, lw["wom"], lw["wog"], lw["wout"], lw["g2"]]
    out_specs = [pl.BlockSpec((1, tm, d), tok), pl.BlockSpec((1, tm, d), tok)]
    out_shape = [jax.ShapeDtypeStruct((b, t, d), F32), jax.ShapeDtypeStruct((b, t, d), BF16)]
    if moe:
        in_specs += [_const_spec((d, LANE)), _const_spec((d, LANE))]
        args += [lw["r_hi"], lw["r_lo"]]
        out_specs.append(pl.BlockSpec((1, tm, LANE), tok))
        out_shape.append(jax.ShapeDtypeStruct((b, t, LANE), F32))
    return pl.pallas_call(
        functools.partial(_post_kernel, moe=moe),
        grid=(t // tm, b),
        in_specs=in_specs,
        out_specs=out_specs,
        out_shape=out_shape,
        input_output_aliases={3: 0},
        compiler_params=_cparams(("parallel", "arbitrary")),
        name="post_moe" if moe else "post_dense",
    )(*args)


def _swiglu_chunk(hb, w1, w3, w2):
    a = _dot(hb, w1)
    g = (a * jax.nn.sigmoid(a)) * _dot(hb, w3)
    return _dot(g.astype(BF16), w2)


def _dense_ffn_kernel(h_ref, x_ref, mod_ref, w1_ref, w3_ref, w2_ref, o_ref, *, fc):
    hb = h_ref[0]
    f = w1_ref.shape[1]
    acc = None
    for lo in range(0, f, fc):
        part = _swiglu_chunk(hb, w1_ref[:, lo:lo + fc], w3_ref[:, lo:lo + fc],
                             w2_ref[lo:lo + fc, :])
        acc = part if acc is None else acc + part
    o_ref[0] = x_ref[0] + mod_ref[0][5:6] * acc


def _dense_ffn_call(h2, xs, mod, lw, *, n_ctx, tm):
    b, t, d = xs.shape
    f = lw["w1"].shape[1]
    nlt = (t - n_ctx) // tm
    ctx_row = b

    def tok(i, bb):
        return (bb, i, 0)

    def modmap(i, bb):
        return (jnp.where(i >= nlt, ctx_row, bb), 0, 0)

    return pl.pallas_call(
        functools.partial(_dense_ffn_kernel, fc=f // 2),
        grid=(t // tm, b),
        in_specs=[
            pl.BlockSpec((1, tm, d), tok),
            pl.BlockSpec((1, tm, d), tok),
            pl.BlockSpec((1, 6, d), modmap),
            pl.BlockSpec((d, f), lambda i, bb: (0, 0), pipeline_mode=pl.Buffered(1)),
            pl.BlockSpec((d, f), lambda i, bb: (0, 0), pipeline_mode=pl.Buffered(1)),
            pl.BlockSpec((f, d), lambda i, bb: (0, 0), pipeline_mode=pl.Buffered(1)),
        ],
        out_specs=pl.BlockSpec((1, tm, d), tok),
        out_shape=jax.ShapeDtypeStruct((b, t, d), F32),
        input_output_aliases={1: 0},
        compiler_params=_cparams(("parallel", "arbitrary")),
        name="ffn_dense",
    )(h2, xs, mod, lw["w1"], lw["w3"], lw["w2"])


MOE_F_SPLIT = 2


def _moe_tile(n):
    for tb in (1024, 768, 512, 256):
        if n % tb == 0:
            return tb
    raise ValueError(n)


def _moe_rows(tb):
    half = tb // N_EXPERTS
    return -(-(half + half // 8) // 16) * 16


def _moe_ffn_kernel(h_ref, comb_ref, tri_ref, w1_ref, w3_ref, w2_ref, o_ref,
                    sel_scr, pos_scr, xg_scr, y_scr, *, rows):
    e = pl.program_id(1)
    fh = pl.program_id(2)
    tb = h_ref.shape[0]

    @pl.when((e == 0) & (fh == 0))
    def _():
        sel_t = (comb_ref[...].T[:16] > 0.0).astype(BF16)
        sel_scr[...] = sel_t.astype(F32)
        pos_scr[...] = lax.dot_general(sel_t, tri_ref[...], (((1,), (1,)), ((), ())),
                                       preferred_element_type=F32)
        o_ref[...] = jnp.zeros_like(o_ref)

    sel_e = sel_scr[pl.ds(e, 1), :]
    rank_e = jnp.where(sel_e > 0.0, pos_scr[pl.ds(e, 1), :], -1.0)
    count = jnp.sum(sel_e).astype(jnp.int32)
    nsub = (count + rows - 1) // rows

    def one_hot(s):
        j = lax.broadcasted_iota(jnp.int32, (rows, tb), 0) + s * rows
        return jnp.where(rank_e == j.astype(F32), 1.0, 0.0).astype(BF16)

    def block(s):
        return pl.ds(pl.multiple_of(s * rows, 16), rows)

    @pl.when(fh == 0)
    def _():
        def gather(s, carry):
            xg_scr[block(s), :] = _dot(one_hot(s), h_ref[...]).astype(BF16)
            return carry
        lax.fori_loop(0, nsub, gather, 0)

    def expert(s, carry):
        part = _swiglu_chunk(xg_scr[block(s), :], w1_ref[0], w3_ref[0], w2_ref[0])

        @pl.when(fh == 0)
        def _():
            y_scr[block(s), :] = part

        @pl.when(fh != 0)
        def _():
            y_scr[block(s), :] = y_scr[block(s), :] + part
        return carry
    lax.fori_loop(0, nsub, expert, 0)

    @pl.when(fh == pl.num_programs(2) - 1)
    def _():
        comb = comb_ref[...]
        lane = lax.broadcasted_iota(jnp.int32, comb.shape, 1)
        w_col = jnp.sum(jnp.where(lane == e, comb, 0.0), axis=-1, keepdims=True)

        def scatter(s, carry):
            z = lax.dot_general(one_hot(s), y_scr[block(s), :].astype(BF16),
                                (((0,), (0,)), ((), ())), preferred_element_type=F32)
            o_ref[...] += w_col * z
            return carry
        lax.fori_loop(0, nsub, scatter, 0)


def _moe_ffn_call(h2, comb, lw):
    n, d = h2.shape
    ne, _, f = lw["w1"].shape
    fc = f // MOE_F_SPLIT
    tb = _moe_tile(n)
    rows = _moe_rows(tb)
    nsub_max = -(-tb // rows)
    tri = (lax.broadcasted_iota(jnp.int32, (tb, tb), 1)
           < lax.broadcasted_iota(jnp.int32, (tb, tb), 0)).astype(BF16)
    return pl.pallas_call(
        functools.partial(_moe_ffn_kernel, rows=rows),
        grid=(n // tb, ne, MOE_F_SPLIT),
        in_specs=[
            pl.BlockSpec((tb, d), lambda i, e, fh: (i, 0)),
            pl.BlockSpec((tb, LANE), lambda i, e, fh: (i, 0)),
            pl.BlockSpec((tb, tb), lambda i, e, fh: (0, 0), pipeline_mode=pl.Buffered(1)),
            pl.BlockSpec((1, d, fc), lambda i, e, fh: (e, 0, fh)),
            pl.BlockSpec((1, d, fc), lambda i, e, fh: (e, 0, fh)),
            pl.BlockSpec((1, fc, d), lambda i, e, fh: (e, fh, 0)),
        ],
        out_specs=pl.BlockSpec((tb, d), lambda i, e, fh: (i, 0)),
        out_shape=jax.ShapeDtypeStruct((n, d), F32),
        scratch_shapes=[pltpu.VMEM((16, tb), F32), pltpu.VMEM((16, tb), F32),
                        pltpu.VMEM((nsub_max * rows, d), BF16),
                        pltpu.VMEM((nsub_max * rows, d), F32)],
        compiler_params=_cparams(("parallel", "arbitrary", "arbitrary")),
        name="ffn_moe",
    )(h2, comb, tri, lw["w1"], lw["w3"], lw["w2"])


def _residual_kernel(x_ref, f_ref, mod_ref, o_ref):
    o_ref[0] = x_ref[0] + mod_ref[0][5:6] * f_ref[0]


def _residual_call(xs, f, mod, *, n_ctx, tm):
    b, t, d = xs.shape
    nlt = (t - n_ctx) // tm
    ctx_row = b

    def tok(i, bb):
        return (bb, i, 0)

    def modmap(i, bb):
        return (jnp.where(i >= nlt, ctx_row, bb), 0, 0)

    return pl.pallas_call(
        _residual_kernel,
        grid=(t // tm, b),
        in_specs=[pl.BlockSpec((1, tm, d), tok), pl.BlockSpec((1, tm, d), tok),
                  pl.BlockSpec((1, 6, d), modmap)],
        out_specs=pl.BlockSpec((1, tm, d), tok),
        out_shape=jax.ShapeDtypeStruct((b, t, d), F32),
        input_output_aliases={0: 0},
        compiler_params=_cparams(("parallel", "arbitrary")),
        name="moe_residual",
    )(xs, f, mod)


def _final_norm_kernel(x_ref, g_ref, o_ref):
    o_ref[0] = _rms(x_ref[0]) * g_ref[...]


def _final_norm_res_kernel(x_ref, g_ref, f_ref, mod_ref, o_ref):
    o_ref[0] = _rms(x_ref[0] + mod_ref[0][5:6] * f_ref[0]) * g_ref[...]


def _final_norm_call(xs, g, *, n_ctx, tm, f=None, mod=None):
    b, t, d = xs.shape
    n_lat = t - n_ctx
    tok = pl.BlockSpec((1, tm, d), lambda i, bb: (bb, i, 0))
    in_specs = [tok, _const_spec((1, d))]
    args = [xs, g]
    kern = _final_norm_kernel
    if f is not None:
        in_specs += [tok, pl.BlockSpec((1, 6, d), lambda i, bb: (bb, 0, 0))]
        args += [f, mod]
        kern = _final_norm_res_kernel
    return pl.pallas_call(
        kern,
        grid=(n_lat // tm, b),
        in_specs=in_specs,
        out_specs=tok,
        out_shape=jax.ShapeDtypeStruct((b, n_lat, d), F32),
        compiler_params=_cparams(("parallel", "arbitrary")),
        name="final_norm",
    )(*args)


def _rope_parts(n_ctx, n_lat, dim):
    half = dim // 2
    quarter = half // 2
    inv = ROPE_THETA ** (-(jnp.arange(quarter, dtype=F32) * 2.0) / half)
    tpos = jnp.arange(n_lat)
    row = (tpos // GRID_W).astype(F32)
    col = (tpos % GRID_W).astype(F32)
    zeros = jnp.zeros((n_ctx, quarter), F32)
    ang_r = jnp.concatenate([row[:, None] * inv, zeros], axis=0)
    ang_c = jnp.concatenate([col[:, None] * inv, zeros], axis=0)
    cr, sr, cc, sc = jnp.cos(ang_r), jnp.sin(ang_r), jnp.cos(ang_c), jnp.sin(ang_c)
    cos = jnp.concatenate([cr, cr, cc, cc], axis=1)
    sin = jnp.concatenate([-sr, sr, -sc, sc], axis=1)
    return cos, sin


def _swap_perm(dim):
    q = dim // 4
    idx = np.arange(dim)
    return np.concatenate([idx[q:2 * q], idx[:q], idx[3 * q:], idx[2 * q:3 * q]])


def _layer_tables(cos_m, sin_m, cos_g, sin_g, gq, gk):
    t = cos_m.shape[0]
    z64 = jnp.zeros((t, 64), F32)
    z32 = jnp.zeros((t, 32), F32)
    sm = MLA_SCALE * LOG2E
    sg = GQA_SCALE * LOG2E
    pg = _swap_perm(GQA_HEAD_DIM)
    am = jnp.concatenate([jnp.full((t, 64), sm, F32), sm * cos_m, z32], axis=1)
    bm = jnp.concatenate([z64, sm * sin_m, z32], axis=1)
    ck = jnp.concatenate([z64, cos_m, z32], axis=1)
    sk = jnp.concatenate([z64, sin_m, z32], axis=1)
    aq = sg * cos_g * gq[None, :]
    bq = sg * sin_g * gq[pg][None, :]
    ak = cos_g * gk[None, :]
    bk = sin_g * gk[pg][None, :]
    two = lambda a: jnp.concatenate([a, a], axis=1)
    return jnp.stack([am, bm, ck, sk, two(aq), two(bq), two(ak), two(bk)], axis=0)


def _layer_weights(l, w_in, mla_w_qb, mla_w_kvb, w_o_mla, w_o_gqa, w_out):
    d = w_in.shape[1]
    wl = w_in[l]
    offs = np.cumsum([0, MLA_Q_LORA, MLA_KV_LORA, MLA_ROPE, 512, 128, 128, d, d])
    qc, kvc, kpe, qg, kg, vg, gm, gg = [wl[:, offs[i]:offs[i + 1]] for i in range(8)]
    pm = _swap_perm(MLA_ROPE)
    pg = _swap_perm(GQA_HEAD_DIM)
    pg8 = np.concatenate([pg + 64 * h for h in range(GQA_HEADS)])
    pg2 = np.concatenate([pg + 64 * h for h in range(GQA_KV_HEADS)])

    def pad_kpe(w):
        return jnp.pad(w, ((0, 0), (64, 32)))

    def dup(w):
        return jnp.concatenate([w[:, :64], w[:, :64], w[:, 64:], w[:, 64:]], axis=1)

    w_ext = jnp.concatenate(
        [qc, kvc, pad_kpe(kpe), pad_kpe(kpe[:, pm]), qg, qg[:, pg8],
         dup(kg), dup(kg[:, pg2]), gm, gg], axis=1).astype(BF16)

    wqb = mla_w_qb[l].reshape(MLA_Q_LORA, MLA_HEADS, MLA_NOPE + MLA_ROPE)
    rope_sw = wqb[:, :, MLA_NOPE:][:, :, pm]
    wqb_pad = jnp.pad(wqb, ((0, 0), (0, 0), (0, 32))).reshape(MLA_Q_LORA, 1024)
    wqbs_pad = jnp.pad(rope_sw, ((0, 0), (0, 0), (64, 32))).reshape(MLA_Q_LORA, 1024)

    wkvb = mla_w_kvb[l].reshape(MLA_KV_LORA, MLA_HEADS, MLA_NOPE + MLA_V)
    wkk = jnp.pad(wkvb[:, :, :MLA_NOPE], ((0, 0), (0, 0), (0, 64))).reshape(MLA_KV_LORA, 1024)
    wkv = wkvb[:, :, MLA_NOPE:].reshape(MLA_KV_LORA, MLA_HEADS * MLA_V)

    def pad_rows(w, heads):
        w = jnp.pad(w.reshape(heads, 64, d), ((0, 0), (0, 64), (0, 0)))
        return w.reshape(heads * LANE, d)

    wom = pad_rows(w_o_mla[l], MLA_HEADS)
    wog = pad_rows(w_o_gqa[l], GQA_HEADS)

    eye = np.kron(np.eye(8, dtype=np.float32), np.full((64, 64), 1.0 / 64, np.float32))
    return {
        "w_in": w_ext,
        "wqb": wqb_pad.astype(BF16),
        "wqbs": wqbs_pad.astype(BF16),
        "wkk": wkk.astype(BF16),
        "wkvt": wkv.T.astype(BF16),
        "wvgt": vg.T.astype(BF16),
        "bd": jnp.asarray(eye, BF16),
        "wom": wom.astype(BF16),
        "wog": wog.astype(BF16),
        "wout": w_out[l].astype(BF16),
    }


def kernel(x, c, ctx, c_ctx, w_ada, b_ada, norm1_g, norm2_g, w_in, mla_q_norm_g, mla_w_qb,
           mla_kv_norm_g, mla_w_kvb, gqa_q_norm_g, gqa_k_norm_g, w_o_mla, w_o_gqa, w_out,
           dense_w1, dense_w3, dense_w2, moe_router, moe_w1, moe_w3, moe_w2, final_norm_g):
    b, n_lat, d = x.shape
    n_ctx = ctx.shape[1]
    depth = w_ada.shape[0]
    t = n_ctx + n_lat
    tm = 256
    assert n_ctx % tm == 0 and n_lat % tm == 0 and n_lat % GRID_W == 0

    rows = -(-(b + 1) // 8) * 8
    cc = jnp.zeros((rows, d), F32).at[:b].set(c).at[b].set(c_ctx)
    mod_all = _ada_call(cc, w_ada, b_ada).reshape(depth, rows, 6, d)

    cos_m, sin_m = _rope_parts(n_ctx, n_lat, MLA_ROPE)
    cos_g, sin_g = _rope_parts(n_ctx, n_lat, GQA_HEAD_DIM)

    xs = jnp.concatenate([x, ctx], axis=1)
    for l in range(depth):
        lw = _layer_weights(l, w_in, mla_w_qb, mla_w_kvb, w_o_mla, w_o_gqa, w_out)
        lw["gq"] = mla_q_norm_g[l][None, :]
        lw["gkv"] = mla_kv_norm_g[l][None, :]
        lw["g2"] = norm2_g[l][None, :]
        tabs = _layer_tables(cos_m, sin_m, cos_g, sin_g, gqa_q_norm_g[l], gqa_k_norm_g[l])
        mod = mod_all[l]
        moe = l % 2 == 1

        qm, km, vm, qg, kg, vg, gates = _inproj_call(
            xs, mod, norm1_g[l][None, :], lw, tabs, n_ctx=n_ctx, tm=tm)
        om = _attn_call(qm, km, vm, gqa=False, n_ctx=n_ctx, tq=256)
        og = _attn_call(qg, kg, vg, gqa=True, n_ctx=n_ctx, tq=256)

        if moe:
            r = jnp.pad(moe_router[l // 2], ((0, 0), (0, LANE - N_EXPERTS)))
            lw["r_hi"] = r.astype(BF16)
            lw["r_lo"] = (r - lw["r_hi"].astype(F32)).astype(BF16)
            xs, h2, comb = _post_call(om, og, gates, xs, mod, lw, n_ctx=n_ctx, tm=tm, moe=True)
            lw["w1"] = moe_w1[l // 2].astype(BF16)
            lw["w3"] = moe_w3[l // 2].astype(BF16)
            lw["w2"] = moe_w2[l // 2].astype(BF16)
            f = _moe_ffn_call(h2.reshape(b * t, d), comb.reshape(b * t, LANE), lw)
            f = f.reshape(b, t, d)
            if l == depth - 1:
                return _final_norm_call(xs, final_norm_g[None, :], n_ctx=n_ctx, tm=tm,
                                        f=f, mod=mod)
            xs = _residual_call(xs, f, mod, n_ctx=n_ctx, tm=tm)
        else:
            xs, h2 = _post_call(om, og, gates, xs, mod, lw, n_ctx=n_ctx, tm=tm, moe=False)
            lw["w1"] = dense_w1[l // 2].astype(BF16)
            lw["w3"] = dense_w3[l // 2].astype(BF16)
            lw["w2"] = dense_w2[l // 2].astype(BF16)
            xs = _dense_ffn_call(h2, xs, mod, lw, n_ctx=n_ctx, tm=tm)

    return _final_norm_call(xs, final_norm_g[None, :], n_ctx=n_ctx, tm=tm)
```

```python
import functools
import math

import numpy as np
import jax
import jax.numpy as jnp
from jax import lax
from jax.experimental import pallas as pl
from jax.experimental.pallas import tpu as pltpu

F32 = jnp.float32
BF16 = jnp.bfloat16

GRID_W = 64
ROPE_THETA = 10000.0
NORM_EPS = 1e-6
MLA_HEADS = 8
MLA_Q_LORA = 384
MLA_KV_LORA = 256
MLA_NOPE = 64
MLA_ROPE = 32
MLA_V = 64
MLA_SCALE = (MLA_NOPE + MLA_ROPE) ** -0.5
GQA_HEADS = 8
GQA_KV_HEADS = 2
GQA_GROUP = GQA_HEADS // GQA_KV_HEADS
GQA_HEAD_DIM = 64
GQA_SCALE = GQA_HEAD_DIM ** -0.5
N_EXPERTS = 8
LOG2E = math.log2(math.e)

LANE = 128
VMEM_LIMIT = 56 * 1024 * 1024

C_QC = 0
C_KVC = C_QC + MLA_Q_LORA
C_KPE = C_KVC + MLA_KV_LORA
C_KPES = C_KPE + LANE
C_QG = C_KPES + LANE
C_QGS = C_QG + 512
C_KG = C_QGS + 512
C_KGS = C_KG + 256
C_GATE = C_KGS + 256
C_END = C_GATE + 2048

T_AM, T_BM, T_CK, T_SK, T_AQ, T_BQ, T_AK, T_BK = range(8)


def _cparams(sem, flags=None):
    return pltpu.CompilerParams(dimension_semantics=sem, vmem_limit_bytes=VMEM_LIMIT,
                                flags=flags)


ATTN_FLAGS = None


def _const_spec(shape):
    nd = len(shape)
    return pl.BlockSpec(shape, lambda *_: (0,) * nd)


def _dot(a, b):
    return jnp.dot(a, b, preferred_element_type=F32)


def _rms(x):
    return x * lax.rsqrt(jnp.mean(x * x, axis=-1, keepdims=True) + NORM_EPS)


def _ada_kernel(cc_ref, w_ref, b_ref, o_ref):
    a = cc_ref[...]
    a = a * jax.nn.sigmoid(a)
    a_hi = a.astype(BF16)
    a_lo = (a - a_hi.astype(F32)).astype(BF16)
    w = w_ref[0]
    w_hi = w.astype(BF16)
    w_lo = (w - w_hi.astype(F32)).astype(BF16)
    acc = _dot(a_hi, w_hi) + _dot(a_lo, w_hi) + _dot(a_hi, w_lo)
    o_ref[0] = acc + b_ref[0]


def _ada_call(cc, w_ada, b_ada):
    depth, d, n = w_ada.shape
    rows = cc.shape[0]
    tn = 1536
    return pl.pallas_call(
        _ada_kernel,
        grid=(depth, n // tn),
        in_specs=[
            pl.BlockSpec((rows, d), lambda l, j: (0, 0)),
            pl.BlockSpec((1, d, tn), lambda l, j: (l, 0, j)),
            pl.BlockSpec((1, 1, tn), lambda l, j: (l, 0, j)),
        ],
        out_specs=pl.BlockSpec((1, rows, tn), lambda l, j: (l, 0, j)),
        out_shape=jax.ShapeDtypeStruct((depth, rows, n), F32),
        compiler_params=_cparams(("arbitrary", "arbitrary")),
        name="ada",
    )(cc, w_ada, b_ada.reshape(depth, 1, n))


def _group_mean_sq(x, bd):
    x2 = x * x
    hi = x2.astype(BF16)
    lo = (x2 - hi.astype(F32)).astype(BF16)
    return _dot(hi, bd) + _dot(lo, bd)


def _tile_lanes(t, n):
    return jnp.concatenate([t] * n, axis=1)


def _dot_nt(a, b):
    return lax.dot_general(a, b, (((1,), (1,)), ((), ())), preferred_element_type=F32)


def _inproj_kernel(x_ref, mod_ref, g1_ref, win_ref, gq_ref, wqb_ref, wqbs_ref, gkv_ref,
                   wkk_ref, wkvt_ref, wvgt_ref, bd_ref, tab_ref,
                   qm_ref, km_ref, vm_ref, qg_ref, kg_ref, vg_ref, gate_ref):
    x = x_ref[0]
    mod = mod_ref[0]
    h = (_rms(x) * g1_ref[...]) * (1.0 + mod[1:2]) + mod[0:1]
    hb = h.astype(BF16)

    def proj(lo, hi):
        return _dot(hb, win_ref[:, lo:hi])

    qcn = (_rms(proj(C_QC, C_KVC)) * gq_ref[...]).astype(BF16)
    q = _dot(qcn, wqb_ref[...])
    qs = _dot(qcn, wqbs_ref[...])
    am = _tile_lanes(tab_ref[T_AM], MLA_HEADS)
    bm = _tile_lanes(tab_ref[T_BM], MLA_HEADS)
    qm_ref[0] = (q * am + qs * bm).astype(BF16)

    kvn = (_rms(proj(C_KVC, C_KPE)) * gkv_ref[...]).astype(BF16)
    kpe = proj(C_KPE, C_KPES) * tab_ref[T_CK] + proj(C_KPES, C_QG) * tab_ref[T_SK]
    kn = _dot(kvn, wkk_ref[...])
    km_ref[0] = (kn + _tile_lanes(kpe, MLA_HEADS)).astype(BF16)
    tm = x.shape[0]
    vm_ref[0] = _dot_nt(wkvt_ref[...], kvn).astype(BF16).reshape(MLA_HEADS, MLA_V, tm)

    bd = bd_ref[...]
    qg = proj(C_QG, C_QGS)
    qgs = proj(C_QGS, C_KG)
    rq = lax.rsqrt(_group_mean_sq(qg, bd) + NORM_EPS)
    aq = _tile_lanes(tab_ref[T_AQ], 4)
    bq = _tile_lanes(tab_ref[T_BQ], 4)
    qg_ref[0] = (rq * (qg * aq + qgs * bq)).astype(BF16)

    kg = proj(C_KG, C_KGS)
    kgs = proj(C_KGS, C_GATE)
    rk = lax.rsqrt(_group_mean_sq(kg, bd[:256, :256]) + NORM_EPS)
    ak = _tile_lanes(tab_ref[T_AK], 2)
    bk = _tile_lanes(tab_ref[T_BK], 2)
    kg_ref[0] = (rk * (kg * ak + kgs * bk)).astype(BF16)
    vg_ref[0] = _dot_nt(wvgt_ref[...], hb).astype(BF16).reshape(
        GQA_KV_HEADS, GQA_HEAD_DIM, tm)

    gate_ref[0] = jax.nn.sigmoid(proj(C_GATE, C_END)).astype(BF16)


def _inproj_call(xs, mod, g1, lw, tabs, *, n_ctx, tm):
    b, t, d = xs.shape
    nlt = (t - n_ctx) // tm
    ctx_row = b

    def tok(i, bb):
        return (bb, i, 0)

    def modmap(i, bb):
        return (jnp.where(i >= nlt, ctx_row, bb), 0, 0)

    def tok_spec(w):
        return pl.BlockSpec((1, tm, w), tok), jax.ShapeDtypeStruct((b, t, w), BF16)

    def vt_spec(heads):
        return (pl.BlockSpec((1, heads, 64, tm), lambda i, bb: (bb, 0, 0, i)),
                jax.ShapeDtypeStruct((b, heads, 64, t), BF16))

    outs = [tok_spec(1024), tok_spec(1024), vt_spec(MLA_HEADS),
            tok_spec(512), tok_spec(256), vt_spec(GQA_KV_HEADS), tok_spec(2048)]
    return pl.pallas_call(
        _inproj_kernel,
        grid=(t // tm, b),
        in_specs=[
            pl.BlockSpec((1, tm, d), tok),
            pl.BlockSpec((1, 6, d), modmap),
            _const_spec((1, d)),
            _const_spec((d, C_END)),
            _const_spec((1, MLA_Q_LORA)),
            _const_spec((MLA_Q_LORA, 1024)),
            _const_spec((MLA_Q_LORA, 1024)),
            _const_spec((1, MLA_KV_LORA)),
            _const_spec((MLA_KV_LORA, 1024)),
            _const_spec((MLA_HEADS * MLA_V, MLA_KV_LORA)),
            _const_spec((GQA_KV_HEADS * GQA_HEAD_DIM, d)),
            _const_spec((512, 512)),
            pl.BlockSpec((8, tm, LANE), lambda i, bb: (0, i, 0)),
        ],
        out_specs=[o[0] for o in outs],
        out_shape=[o[1] for o in outs],
        compiler_params=_cparams(("parallel", "arbitrary")),
        name="inproj",
    )(xs, mod, g1, lw["w_in"], lw["gq"], lw["wqb"], lw["wqbs"], lw["gkv"],
      lw["wkk"], lw["wkvt"], lw["wvgt"], lw["bd"], tabs)


def _key_block(nkeys):
    for kb in (768, 512, 256):
        if nkeys % kb == 0:
            return kb
    raise ValueError(nkeys)


def _mask_gqa_q(q):
    lane = lax.broadcasted_iota(jnp.int32, q.shape, 1)
    keep = (lane // 64) == (pl.program_id(1) % 2)
    return jnp.where(keep, q, jnp.zeros_like(q))


def _attn_scores(q, k_ref, s_ref, m_ref, nkeys, probs=None, zero_ref=None):
    kb = _key_block(nkeys)
    mrun = lrun = None
    if probs is not None:
        s2_ref, p2_ref, m2_ref, l2_ref = probs
        m2 = m2_ref[0:1, :]
    for lo in range(0, nkeys, kb):
        k = k_ref[0, lo:lo + kb, :]
        if probs is not None:
            p = jnp.exp2(s2_ref[lo:lo + kb, :] - m2)
            f = jnp.sum(p, axis=0, keepdims=True)
            lrun = f if lrun is None else lrun + f
            p2_ref[lo:lo + kb, :] = p.astype(BF16)
            tie = pltpu.bitcast(f, jnp.uint32) & zero_ref[0:1, :]
            tie = tie[:, :LANE] | tie[:, LANE:]
            k = pltpu.bitcast(pltpu.bitcast(k, jnp.uint32) | tie, BF16)
        s = lax.dot_general(k, q, (((1,), (1,)), ((), ())), preferred_element_type=F32)
        s_ref[lo:lo + kb, :] = s
        f = jnp.max(s, axis=0, keepdims=True)
        mrun = f if mrun is None else jnp.maximum(mrun, f)
    m_ref[...] = jnp.broadcast_to(mrun, m_ref.shape)
    if probs is not None:
        l2_ref[...] = jnp.broadcast_to(lrun, l2_ref.shape)


def _attn_probs(s_ref, p_ref, m_ref, l_ref, nkeys):
    kb = _key_block(nkeys)
    m = m_ref[0:1, :]
    lrun = None
    for lo in range(0, nkeys, kb):
        p = jnp.exp2(s_ref[lo:lo + kb, :] - m)
        f = jnp.sum(p, axis=0, keepdims=True)
        lrun = f if lrun is None else lrun + f
        p_ref[lo:lo + kb, :] = p.astype(BF16)
    l_ref[...] = jnp.broadcast_to(lrun, l_ref.shape)


def _attn_values(p_ref, vt_ref, l_ref, nkeys):
    kb = _key_block(nkeys)
    acc = None
    for lo in range(0, nkeys, kb):
        part = _dot(vt_ref[0, 0, :, lo:lo + kb], p_ref[lo:lo + kb, :])
        acc = part if acc is None else acc + part
    o_t = acc / l_ref[0:1, :]
    o_t = jnp.concatenate([o_t, jnp.zeros_like(o_t)], axis=0)
    return o_t.T.astype(BF16)


def _attn_ctx_kernel(q_ref, k_ref, vt_ref, o_ref, s_scr, p_scr, m_scr, l_scr, *, gqa):
    q = q_ref[0]
    if gqa:
        q = _mask_gqa_q(q)
    nkeys = k_ref.shape[1]
    _attn_scores(q, k_ref, s_scr, m_scr, nkeys)
    _attn_probs(s_scr, p_scr, m_scr, l_scr, nkeys)
    o_ref[0] = _attn_values(p_scr, vt_ref, l_scr, nkeys)


def _attn_lat_kernel(q_ref, k_ref, vt_ref, zero_ref, oin_ref, o_ref,
                     s0, s1, p0, p1, m0, m1, l0, l1, *, gqa, tq, nkeys):
    del oin_ref
    j = pl.program_id(2)
    last = pl.num_programs(2) - 1
    s_scr, p_scr, m_scr, l_scr = (s0, s1), (p0, p1), (m0, m1), (l0, l1)

    def scores(slot, tied_probs=False):
        q = q_ref[0, slot * tq:(slot + 1) * tq, :]
        if gqa:
            q = _mask_gqa_q(q)
        other = None
        if tied_probs:
            o = 1 - slot
            other = (s_scr[o], p_scr[o], m_scr[o], l_scr[o])
        _attn_scores(q, k_ref, s_scr[slot], m_scr[slot], nkeys, other, zero_ref)

    def probs(slot):
        _attn_probs(s_scr[slot], p_scr[slot], m_scr[slot], l_scr[slot], nkeys)

    def values(slot):
        o_ref[0, slot * tq:(slot + 1) * tq, :] = _attn_values(
            p_scr[slot], vt_ref, l_scr[slot], nkeys)

    @pl.when(j == 0)
    def _():
        scores(0)
        scores(1)
        probs(0)

    @pl.when((j > 0) & (j < last))
    def _():
        scores(0, tied_probs=True)
        values(0)
        scores(1)
        probs(0)
        values(1)

    @pl.when(j == last)
    def _():
        probs(1)
        values(0)
        values(1)


def _attn_call(q, k, vt, *, gqa, n_ctx, tq):
    b, t, _ = q.shape
    n_lat = t - n_ctx
    heads = GQA_HEADS if gqa else MLA_HEADS
    if gqa:
        qcol = lambda h: h // 2
        kcol = lambda h: h // GQA_GROUP
        vhead = kcol
    else:
        qcol = lambda h: h
        kcol = lambda h: h
        vhead = lambda h: h
    out_shape = jax.ShapeDtypeStruct((b, t, heads * LANE), BF16)
    sem = ("parallel", "parallel", "arbitrary")

    cblk = n_lat // n_ctx
    o_ctx = pl.pallas_call(
        functools.partial(_attn_ctx_kernel, gqa=gqa),
        grid=(b, heads),
        in_specs=[
            pl.BlockSpec((1, n_ctx, LANE), lambda bb, h: (bb, cblk, qcol(h))),
            pl.BlockSpec((1, n_ctx, LANE), lambda bb, h: (bb, cblk, kcol(h))),
            pl.BlockSpec((1, 1, 64, n_ctx), lambda bb, h: (bb, vhead(h), 0, cblk)),
        ],
        out_specs=pl.BlockSpec((1, n_ctx, LANE), lambda bb, h: (bb, cblk, h)),
        out_shape=out_shape,
        scratch_shapes=[pltpu.VMEM((n_ctx, n_ctx), F32), pltpu.VMEM((n_ctx, n_ctx), BF16),
                        pltpu.VMEM((8, n_ctx), F32), pltpu.VMEM((8, n_ctx), F32)],
        compiler_params=_cparams(sem[:2]),
        name="attn_ctx_gqa" if gqa else "attn_ctx_mla",
    )(q, k, vt)

    npairs = n_lat // (2 * tq)
    kern = functools.partial(_attn_lat_kernel, gqa=gqa, tq=tq, nkeys=t)
    return pl.pallas_call(
        kern,
        grid=(b, heads, npairs + 1),
        in_specs=[
            pl.BlockSpec((1, 2 * tq, LANE),
                         lambda bb, h, j: (bb, jnp.minimum(j, npairs - 1), qcol(h))),
            pl.BlockSpec((1, t, LANE), lambda bb, h, j: (bb, 0, kcol(h))),
            pl.BlockSpec((1, 1, 64, t), lambda bb, h, j: (bb, vhead(h), 0, 0)),
            pl.BlockSpec((8, tq), lambda bb, h, j: (0, 0)),
            pl.BlockSpec(memory_space=pl.ANY),
        ],
        out_specs=pl.BlockSpec((1, 2 * tq, LANE),
                               lambda bb, h, j: (bb, jnp.maximum(j - 1, 0), h)),
        out_shape=out_shape,
        scratch_shapes=[pltpu.VMEM((t, tq), F32)] * 2 + [pltpu.VMEM((t, tq), BF16)] * 2
        + [pltpu.VMEM((8, tq), F32)] * 4,
        input_output_aliases={4: 0},
        compiler_params=_cparams(sem, ATTN_FLAGS),
        name="attn_gqa" if gqa else "attn_mla",
    )(q, k, vt, jnp.zeros((8, tq), jnp.uint32), o_ctx)


def _top2_combine(logits):
    lane = lax.broadcasted_iota(jnp.int32, logits.shape, 1)
    neg = jnp.float32(-jnp.inf)
    lg = jnp.where(lane < N_EXPERTS, logits, neg)
    m1 = jnp.max(lg, axis=-1, keepdims=True)
    i1 = jnp.min(jnp.where(lg == m1, lane, LANE), axis=-1, keepdims=True)
    sel1 = lane == i1
    lg2 = jnp.where(sel1, neg, lg)
    m2 = jnp.max(lg2, axis=-1, keepdims=True)
    i2 = jnp.min(jnp.where(lg2 == m2, lane, LANE), axis=-1, keepdims=True)
    sel2 = lane == i2
    e2 = jnp.exp(m2 - m1)
    w1 = 1.0 / (1.0 + e2)
    w2 = e2 / (1.0 + e2)
    return jnp.where(sel1, w1, 0.0) + jnp.where(sel2, w2, 0.0)


def _post_kernel(*refs, moe):
    if moe:
        (om_ref, og_ref, gate_ref, x_ref, mod_ref, wom_ref, wog_ref, wout_ref, g2_ref,
         rhi_ref, rlo_ref, xo_ref, h2_ref, comb_ref) = refs
    else:
        (om_ref, og_ref, gate_ref, x_ref, mod_ref, wom_ref, wog_ref, wout_ref, g2_ref,
         xo_ref, h2_ref) = refs
    a = _dot(om_ref[0], wom_ref[...])
    bb = _dot(og_ref[0], wog_ref[...])
    gate = gate_ref[0]
    merged = gate[:, :1024].astype(F32) * a + gate[:, 1024:].astype(F32) * bb
    mix = _dot(merged.astype(BF16), wout_ref[...])
    mod = mod_ref[0]
    x1 = x_ref[0] + mod[2:3] * mix
    xo_ref[0] = x1
    h2 = (_rms(x1) * g2_ref[...]) * (1.0 + mod[4:5]) + mod[3:4]
    h2_ref[0] = h2.astype(BF16)
    if moe:
        h_hi = h2.astype(BF16)
        h_lo = (h2 - h_hi.astype(F32)).astype(BF16)
        logits = (_dot(h_hi, rhi_ref[...]) + _dot(h_lo, rhi_ref[...])
                  + _dot(h_hi, rlo_ref[...]))
        comb_ref[0] = _top2_combine(logits)


def _post_call(om, og, gates, xs, mod, lw, *, n_ctx, tm, moe):
    b, t, d = xs.shape
    nlt = (t - n_ctx) // tm
    ctx_row = b

    def tok(i, bb):
        return (bb, i, 0)

    def modmap(i, bb):
        return (jnp.where(i >= nlt, ctx_row, bb), 0, 0)

    in_specs = [
        pl.BlockSpec((1, tm, MLA_HEADS * LANE), tok),
        pl.BlockSpec((1, tm, GQA_HEADS * LANE), tok),
        pl.BlockSpec((1, tm, 2048), tok),
        pl.BlockSpec((1, tm, d), tok),
        pl.BlockSpec((1, 6, d), modmap),
        _const_spec((MLA_HEADS * LANE, d)),
        _const_spec((GQA_HEADS * LANE, d)),
        _const_spec((d, d)),
        _const_spec((1, d)),
    ]
    args = [om, og, gates, xs, mod, lw["wom"], lw["wog"], lw["wout"], lw["g2"]]
    out_specs = [pl.BlockSpec((1, tm, d), tok), pl.BlockSpec((1, tm, d), tok)]
    out_shape = [jax.ShapeDtypeStruct((b, t, d), F32), jax.ShapeDtypeStruct((b, t, d), BF16)]
    if moe:
        in_specs += [_const_spec((d, LANE)), _const_spec((d, LANE))]
        args += [lw["r_hi"], lw["r_lo"]]
        out_specs.append(pl.BlockSpec((1, tm, LANE), tok))
        out_shape.append(jax.ShapeDtypeStruct((b, t, LANE), F32))
    return pl.pallas_call(
        functools.partial(_post_kernel, moe=moe),
        grid=(t // tm, b),
        in_specs=in_specs,
        out_specs=out_specs,
        out_shape=out_shape,
        input_output_aliases={3: 0},
        compiler_params=_cparams(("parallel", "arbitrary")),
        name="post_moe" if moe else "post_dense",
    )(*args)


def _swiglu_chunk(hb, w1, w3, w2):
    a = _dot(hb, w1)
    g = (a * jax.nn.sigmoid(a)) * _dot(hb, w3)
    return _dot(g.astype(BF16), w2)


def _dense_ffn_kernel(h_ref, x_ref, mod_ref, w1_ref, w3_ref, w2_ref, o_ref, *, fc):
    hb = h_ref[0]
    f = w1_ref.shape[1]
    acc = None
    for lo in range(0, f, fc):
        part = _swiglu_chunk(hb, w1_ref[:, lo:lo + fc], w3_ref[:, lo:lo + fc],
                             w2_ref[lo:lo + fc, :])
        acc = part if acc is None else acc + part
    o_ref[0] = x_ref[0] + mod_ref[0][5:6] * acc


def _dense_ffn_call(h2, xs, mod, lw, *, n_ctx, tm):
    b, t, d = xs.shape
    f = lw["w1"].shape[1]
    nlt = (t - n_ctx) // tm
    ctx_row = b

    def tok(i, bb):
        return (bb, i, 0)

    def modmap(i, bb):
        return (jnp.where(i >= nlt, ctx_row, bb), 0, 0)

    return pl.pallas_call(
        functools.partial(_dense_ffn_kernel, fc=f // 2),
        grid=(t // tm, b),
        in_specs=[
            pl.BlockSpec((1, tm, d), tok),
            pl.BlockSpec((1, tm, d), tok),
            pl.BlockSpec((1, 6, d), modmap),
            pl.BlockSpec((d, f), lambda i, bb: (0, 0), pipeline_mode=pl.Buffered(1)),
            pl.BlockSpec((d, f), lambda i, bb: (0, 0), pipeline_mode=pl.Buffered(1)),
            pl.BlockSpec((f, d), lambda i, bb: (0, 0), pipeline_mode=pl.Buffered(1)),
        ],
        out_specs=pl.BlockSpec((1, tm, d), tok),
        out_shape=jax.ShapeDtypeStruct((b, t, d), F32),
        input_output_aliases={1: 0},
        compiler_params=_cparams(("parallel", "arbitrary")),
        name="ffn_dense",
    )(h2, xs, mod, lw["w1"], lw["w3"], lw["w2"])


MOE_F_SPLIT = 2


def _moe_tile(n):
    for tb in (1024, 768, 512, 256):
        if n % tb == 0:
            return tb
    raise ValueError(n)


def _moe_rows(tb):
    half = tb // N_EXPERTS
    return -(-(half + half // 8) // 16) * 16


def _moe_ffn_kernel(h_ref, comb_ref, tri_ref, w1_ref, w3_ref, w2_ref, o_ref,
                    sel_scr, pos_scr, xg_scr, y_scr, *, rows):
    e = pl.program_id(1)
    fh = pl.program_id(2)
    tb = h_ref.shape[0]

    @pl.when((e == 0) & (fh == 0))
    def _():
        sel_t = (comb_ref[...].T[:16] > 0.0).astype(BF16)
        sel_scr[...] = sel_t.astype(F32)
        pos_scr[...] = lax.dot_general(sel_t, tri_ref[...], (((1,), (1,)), ((), ())),
                                       preferred_element_type=F32)
        o_ref[...] = jnp.zeros_like(o_ref)

    sel_e = sel_scr[pl.ds(e, 1), :]
    rank_e = jnp.where(sel_e > 0.0, pos_scr[pl.ds(e, 1), :], -1.0)
    count = jnp.sum(sel_e).astype(jnp.int32)
    nsub = (count + rows - 1) // rows

    def one_hot(s):
        j = lax.broadcasted_iota(jnp.int32, (rows, tb), 0) + s * rows
        return jnp.where(rank_e == j.astype(F32), 1.0, 0.0).astype(BF16)

    def block(s):
        return pl.ds(pl.multiple_of(s * rows, 16), rows)

    @pl.when(fh == 0)
    def _():
        def gather(s, carry):
            xg_scr[block(s), :] = _dot(one_hot(s), h_ref[...]).astype(BF16)
            return carry
        lax.fori_loop(0, nsub, gather, 0)

    def expert(s, carry):
        part = _swiglu_chunk(xg_scr[block(s), :], w1_ref[0], w3_ref[0], w2_ref[0])

        @pl.when(fh == 0)
        def _():
            y_scr[block(s), :] = part

        @pl.when(fh != 0)
        def _():
            y_scr[block(s), :] = y_scr[block(s), :] + part
        return carry
    lax.fori_loop(0, nsub, expert, 0)

    @pl.when(fh == pl.num_programs(2) - 1)
    def _():
        comb = comb_ref[...]
        lane = lax.broadcasted_iota(jnp.int32, comb.shape, 1)
        w_col = jnp.sum(jnp.where(lane == e, comb, 0.0), axis=-1, keepdims=True)

        def scatter(s, carry):
            z = lax.dot_general(one_hot(s), y_scr[block(s), :].astype(BF16),
                                (((0,), (0,)), ((), ())), preferred_element_type=F32)
            o_ref[...] += w_col * z
            return carry
        lax.fori_loop(0, nsub, scatter, 0)


def _moe_ffn_call(h2, comb, lw):
    n, d = h2.shape
    ne, _, f = lw["w1"].shape
    fc = f // MOE_F_SPLIT
    tb = _moe_tile(n)
    rows = _moe_rows(tb)
    nsub_max = -(-tb // rows)
    tri = (lax.broadcasted_iota(jnp.int32, (tb, tb), 1)
           < lax.broadcasted_iota(jnp.int32, (tb, tb), 0)).astype(BF16)
    return pl.pallas_call(
        functools.partial(_moe_ffn_kernel, rows=rows),
        grid=(n // tb, ne, MOE_F_SPLIT),
        in_specs=[
            pl.BlockSpec((tb, d), lambda i, e, fh: (i, 0)),
            pl.BlockSpec((tb, LANE), lambda i, e, fh: (i, 0)),
            pl.BlockSpec((tb, tb), lambda i, e, fh: (0, 0), pipeline_mode=pl.Buffered(1)),
            pl.BlockSpec((1, d, fc), lambda i, e, fh: (e, 0, fh)),
            pl.BlockSpec((1, d, fc), lambda i, e, fh: (e, 0, fh)),
            pl.BlockSpec((1, fc, d), lambda i, e, fh: (e, fh, 0)),
        ],
        out_specs=pl.BlockSpec((tb, d), lambda i, e, fh: (i, 0)),
        out_shape=jax.ShapeDtypeStruct((n, d), F32),
        scratch_shapes=[pltpu.VMEM((16, tb), F32), pltpu.VMEM((16, tb), F32),
                        pltpu.VMEM((nsub_max * rows, d), BF16),
                        pltpu.VMEM((nsub_max * rows, d), F32)],
        compiler_params=_cparams(("parallel", "arbitrary", "arbitrary")),
        name="ffn_moe",
    )(h2, comb, tri, lw["w1"], lw["w3"], lw["w2"])


def _residual_kernel(x_ref, f_ref, mod_ref, o_ref):
    o_ref[0] = x_ref[0] + mod_ref[0][5:6] * f_ref[0]


def _residual_call(xs, f, mod, *, n_ctx, tm):
    b, t, d = xs.shape
    nlt = (t - n_ctx) // tm
    ctx_row = b

    def tok(i, bb):
        return (bb, i, 0)

    def modmap(i, bb):
        return (jnp.where(i >= nlt, ctx_row, bb), 0, 0)

    return pl.pallas_call(
        _residual_kernel,
        grid=(t // tm, b),
        in_specs=[pl.BlockSpec((1, tm, d), tok), pl.BlockSpec((1, tm, d), tok),
                  pl.BlockSpec((1, 6, d), modmap)],
        out_specs=pl.BlockSpec((1, tm, d), tok),
        out_shape=jax.ShapeDtypeStruct((b, t, d), F32),
        input_output_aliases={0: 0},
        compiler_params=_cparams(("parallel", "arbitrary")),
        name="moe_residual",
    )(xs, f, mod)


def _final_norm_kernel(x_ref, g_ref, o_ref):
    o_ref[0] = _rms(x_ref[0]) * g_ref[...]


def _final_norm_res_kernel(x_ref, g_ref, f_ref, mod_ref, o_ref):
    o_ref[0] = _rms(x_ref[0] + mod_ref[0][5:6] * f_ref[0]) * g_ref[...]


def _final_norm_call(xs, g, *, n_ctx, tm, f=None, mod=None):
    b, t, d = xs.shape
    n_lat = t - n_ctx
    tok = pl.BlockSpec((1, tm, d), lambda i, bb: (bb, i, 0))
    in_specs = [tok, _const_spec((1, d))]
    args = [xs, g]
    kern = _final_norm_kernel
    if f is not None:
        in_specs += [tok, pl.BlockSpec((1, 6, d), lambda i, bb: (bb, 0, 0))]
        args += [f, mod]
        kern = _final_norm_res_kernel
    return pl.pallas_call(
        kern,
        grid=(n_lat // tm, b),
        in_specs=in_specs,
        out_specs=tok,
        out_shape=jax.ShapeDtypeStruct((b, n_lat, d), F32),
        compiler_params=_cparams(("parallel", "arbitrary")),
        name="final_norm",
    )(*args)


def _rope_parts(n_ctx, n_lat, dim):
    half = dim // 2
    quarter = half // 2
    inv = ROPE_THETA ** (-(jnp.arange(quarter, dtype=F32) * 2.0) / half)
    tpos = jnp.arange(n_lat)
    row = (tpos // GRID_W).astype(F32)
    col = (tpos % GRID_W).astype(F32)
    zeros = jnp.zeros((n_ctx, quarter), F32)
    ang_r = jnp.concatenate([row[:, None] * inv, zeros], axis=0)
    ang_c = jnp.concatenate([col[:, None] * inv, zeros], axis=0)
    cr, sr, cc, sc = jnp.cos(ang_r), jnp.sin(ang_r), jnp.cos(ang_c), jnp.sin(ang_c)
    cos = jnp.concatenate([cr, cr, cc, cc], axis=1)
    sin = jnp.concatenate([-sr, sr, -sc, sc], axis=1)
    return cos, sin


def _swap_perm(dim):
    q = dim // 4
    idx = np.arange(dim)
    return np.concatenate([idx[q:2 * q], idx[:q], idx[3 * q:], idx[2 * q:3 * q]])


def _layer_tables(cos_m, sin_m, cos_g, sin_g, gq, gk):
    t = cos_m.shape[0]
    z64 = jnp.zeros((t, 64), F32)
    z32 = jnp.zeros((t, 32), F32)
    sm = MLA_SCALE * LOG2E
    sg = GQA_SCALE * LOG2E
    pg = _swap_perm(GQA_HEAD_DIM)
    am = jnp.concatenate([jnp.full((t, 64), sm, F32), sm * cos_m, z32], axis=1)
    bm = jnp.concatenate([z64, sm * sin_m, z32], axis=1)
    ck = jnp.concatenate([z64, cos_m, z32], axis=1)
    sk = jnp.concatenate([z64, sin_m, z32], axis=1)
    aq = sg * cos_g * gq[None, :]
    bq = sg * sin_g * gq[pg][None, :]
    ak = cos_g * gk[None, :]
    bk = sin_g * gk[pg][None, :]
    two = lambda a: jnp.concatenate([a, a], axis=1)
    return jnp.stack([am, bm, ck, sk, two(aq), two(bq), two(ak), two(bk)], axis=0)


def _layer_weights(l, w_in, mla_w_qb, mla_w_kvb, w_o_mla, w_o_gqa, w_out):
    d = w_in.shape[1]
    wl = w_in[l]
    offs = np.cumsum([0, MLA_Q_LORA, MLA_KV_LORA, MLA_ROPE, 512, 128, 128, d, d])
    qc, kvc, kpe, qg, kg, vg, gm, gg = [wl[:, offs[i]:offs[i + 1]] for i in range(8)]
    pm = _swap_perm(MLA_ROPE)
    pg = _swap_perm(GQA_HEAD_DIM)
    pg8 = np.concatenate([pg + 64 * h for h in range(GQA_HEADS)])
    pg2 = np.concatenate([pg + 64 * h for h in range(GQA_KV_HEADS)])

    def pad_kpe(w):
        return jnp.pad(w, ((0, 0), (64, 32)))

    def dup(w):
        return jnp.concatenate([w[:, :64], w[:, :64], w[:, 64:], w[:, 64:]], axis=1)

    w_ext = jnp.concatenate(
        [qc, kvc, pad_kpe(kpe), pad_kpe(kpe[:, pm]), qg, qg[:, pg8],
         dup(kg), dup(kg[:, pg2]), gm, gg], axis=1).astype(BF16)

    wqb = mla_w_qb[l].reshape(MLA_Q_LORA, MLA_HEADS, MLA_NOPE + MLA_ROPE)
    rope_sw = wqb[:, :, MLA_NOPE:][:, :, pm]
    wqb_pad = jnp.pad(wqb, ((0, 0), (0, 0), (0, 32))).reshape(MLA_Q_LORA, 1024)
    wqbs_pad = jnp.pad(rope_sw, ((0, 0), (0, 0), (64, 32))).reshape(MLA_Q_LORA, 1024)

    wkvb = mla_w_kvb[l].reshape(MLA_KV_LORA, MLA_HEADS, MLA_NOPE + MLA_V)
    wkk = jnp.pad(wkvb[:, :, :MLA_NOPE], ((0, 0), (0, 0), (0, 64))).reshape(MLA_KV_LORA, 1024)
    wkv = wkvb[:, :, MLA_NOPE:].reshape(MLA_KV_LORA, MLA_HEADS * MLA_V)

    def pad_rows(w, heads):
        w = jnp.pad(w.reshape(heads, 64, d), ((0, 0), (0, 64), (0, 0)))
        return w.reshape(heads * LANE, d)

    wom = pad_rows(w_o_mla[l], MLA_HEADS)
    wog = pad_rows(w_o_gqa[l], GQA_HEADS)

    eye = np.kron(np.eye(8, dtype=np.float32), np.full((64, 64), 1.0 / 64, np.float32))
    return {
        "w_in": w_ext,
        "wqb": wqb_pad.astype(BF16),
        "wqbs": wqbs_pad.astype(BF16),
        "wkk": wkk.astype(BF16),
        "wkvt": wkv.T.astype(BF16),
        "wvgt": vg.T.astype(BF16),
        "bd": jnp.asarray(eye, BF16),
        "wom": wom.astype(BF16),
        "wog": wog.astype(BF16),
        "wout": w_out[l].astype(BF16),
    }


def kernel(x, c, ctx, c_ctx, w_ada, b_ada, norm1_g, norm2_g, w_in, mla_q_norm_g, mla_w_qb,
           mla_kv_norm_g, mla_w_kvb, gqa_q_norm_g, gqa_k_norm_g, w_o_mla, w_o_gqa, w_out,
           dense_w1, dense_w3, dense_w2, moe_router, moe_w1, moe_w3, moe_w2, final_norm_g):
    b, n_lat, d = x.shape
    n_ctx = ctx.shape[1]
    depth = w_ada.shape[0]
    t = n_ctx + n_lat
    tm = 256
    assert n_ctx % tm == 0 and n_lat % tm == 0 and n_lat % GRID_W == 0

    rows = -(-(b + 1) // 8) * 8
    cc = jnp.zeros((rows, d), F32).at[:b].set(c).at[b].set(c_ctx)
    mod_all = _ada_call(cc, w_ada, b_ada).reshape(depth, rows, 6, d)

    cos_m, sin_m = _rope_parts(n_ctx, n_lat, MLA_ROPE)
    cos_g, sin_g = _rope_parts(n_ctx, n_lat, GQA_HEAD_DIM)

    xs = jnp.concatenate([x, ctx], axis=1)
    for l in range(depth):
        lw = _layer_weights(l, w_in, mla_w_qb, mla_w_kvb, w_o_mla, w_o_gqa, w_out)
        lw["gq"] = mla_q_norm_g[l][None, :]
        lw["gkv"] = mla_kv_norm_g[l][None, :]
        lw["g2"] = norm2_g[l][None, :]
        tabs = _layer_tables(cos_m, sin_m, cos_g, sin_g, gqa_q_norm_g[l], gqa_k_norm_g[l])
        mod = mod_all[l]
        moe = l % 2 == 1

        qm, km, vm, qg, kg, vg, gates = _inproj_call(
            xs, mod, norm1_g[l][None, :], lw, tabs, n_ctx=n_ctx, tm=tm)
        om = _attn_call(qm, km, vm, gqa=False, n_ctx=n_ctx, tq=256)
        og = _attn_call(qg, kg, vg, gqa=True, n_ctx=n_ctx, tq=256)

        if moe:
            r = jnp.pad(moe_router[l // 2], ((0, 0), (0, LANE - N_EXPERTS)))
            lw["r_hi"] = r.astype(BF16)
            lw["r_lo"] = (r - lw["r_hi"].astype(F32)).astype(BF16)
            xs, h2, comb = _post_call(om, og, gates, xs, mod, lw, n_ctx=n_ctx, tm=tm, moe=True)
            lw["w1"] = moe_w1[l // 2].astype(BF16)
            lw["w3"] = moe_w3[l // 2].astype(BF16)
            lw["w2"] = moe_w2[l // 2].astype(BF16)
            f = _moe_ffn_call(h2.reshape(b * t, d), comb.reshape(b * t, LANE), lw)
            f = f.reshape(b, t, d)
            if l == depth - 1:
                return _final_norm_call(xs, final_norm_g[None, :], n_ctx=n_ctx, tm=tm,
                                        f=f, mod=mod)
            xs = _residual_call(xs, f, mod, n_ctx=n_ctx, tm=tm)
        else:
            xs, h2 = _post_call(om, og, gates, xs, mod, lw, n_ctx=n_ctx, tm=tm, moe=False)
            lw["w1"] = dense_w1[l // 2].astype(BF16)
            lw["w3"] = dense_w3[l // 2].astype(BF16)
            lw["w2"] = dense_w2[l // 2].astype(BF16)
            xs = _dense_ffn_call(h2, xs, mod, lw, n_ctx=n_ctx, tm=tm)

    return _final_norm_call(xs, final_norm_g[None, :], n_ctx=n_ctx, tm=tm)
```

```python
import functools
import math

import numpy as np
import jax
import jax.numpy as jnp
from jax import lax
from jax.experimental import pallas as pl
from jax.experimental.pallas import tpu as pltpu

F32 = jnp.float32
BF16 = jnp.bfloat16

GRID_W = 64
ROPE_THETA = 10000.0
NORM_EPS = 1e-6
MLA_HEADS = 8
MLA_Q_LORA = 384
MLA_KV_LORA = 256
MLA_NOPE = 64
MLA_ROPE = 32
MLA_V = 64
MLA_SCALE = (MLA_NOPE + MLA_ROPE) ** -0.5
GQA_HEADS = 8
GQA_KV_HEADS = 2
GQA_GROUP = GQA_HEADS // GQA_KV_HEADS
GQA_HEAD_DIM = 64
GQA_SCALE = GQA_HEAD_DIM ** -0.5
N_EXPERTS = 8
LOG2E = math.log2(math.e)

LANE = 128
VMEM_LIMIT = 56 * 1024 * 1024
TOKEN_TILE = 256
QUERY_TILE = 256

C_QC = 0
C_KVC = C_QC + MLA_Q_LORA
C_KPE = C_KVC + MLA_KV_LORA
C_KPES = C_KPE + LANE
C_QG = C_KPES + LANE
C_QGS = C_QG + 512
C_KG = C_QGS + 512
C_KGS = C_KG + 256
C_GATE = C_KGS + 256
C_END = C_GATE + 2048

T_AM, T_BM, T_CK, T_SK, T_AQ, T_BQ, T_AK, T_BK = range(8)


def _cparams(sem):
    return pltpu.CompilerParams(dimension_semantics=sem, vmem_limit_bytes=VMEM_LIMIT)


def _const_spec(shape):
    nd = len(shape)
    return pl.BlockSpec(shape, lambda *_: (0,) * nd)


def _dot(a, b):
    return jnp.dot(a, b, preferred_element_type=F32)


def _rms(x):
    return x * lax.rsqrt(jnp.mean(x * x, axis=-1, keepdims=True) + NORM_EPS)


def _ada_kernel(cc_ref, w_ref, b_ref, o_ref):
    a = cc_ref[...]
    a = a * jax.nn.sigmoid(a)
    a_hi = a.astype(BF16)
    a_lo = (a - a_hi.astype(F32)).astype(BF16)
    w = w_ref[0]
    w_hi = w.astype(BF16)
    w_lo = (w - w_hi.astype(F32)).astype(BF16)
    acc = _dot(a_hi, w_hi) + _dot(a_lo, w_hi) + _dot(a_hi, w_lo)
    o_ref[0] = acc + b_ref[0]


def _ada_call(cc, w_ada, b_ada):
    depth, d, n = w_ada.shape
    rows = cc.shape[0]
    tn = 1536
    return pl.pallas_call(
        _ada_kernel,
        grid=(depth, n // tn),
        in_specs=[
            pl.BlockSpec((rows, d), lambda l, j: (0, 0)),
            pl.BlockSpec((1, d, tn), lambda l, j: (l, 0, j)),
            pl.BlockSpec((1, 1, tn), lambda l, j: (l, 0, j)),
        ],
        out_specs=pl.BlockSpec((1, rows, tn), lambda l, j: (l, 0, j)),
        out_shape=jax.ShapeDtypeStruct((depth, rows, n), F32),
        compiler_params=_cparams(("arbitrary", "arbitrary")),
        name="ada",
    )(cc, w_ada, b_ada.reshape(depth, 1, n))


def _group_mean_sq(x, bd):
    x2 = x * x
    hi = x2.astype(BF16)
    lo = (x2 - hi.astype(F32)).astype(BF16)
    return _dot(hi, bd) + _dot(lo, bd)


def _tile_lanes(t, n):
    return jnp.concatenate([t] * n, axis=1)


def _dot_nt(a, b):
    return lax.dot_general(a, b, (((1,), (1,)), ((), ())), preferred_element_type=F32)


def _inproj_kernel(x_ref, mod_ref, g1_ref, win_ref, gq_ref, wqb_ref, wqbs_ref, gkv_ref,
                   wkk_ref, wkvt_ref, wvgt_ref, bd_ref, tab_ref,
                   qm_ref, km_ref, vm_ref, qg_ref, kg_ref, vg_ref, gate_ref):
    x = x_ref[0]
    mod = mod_ref[0]
    h = (_rms(x) * g1_ref[...]) * (1.0 + mod[1:2]) + mod[0:1]
    hb = h.astype(BF16)

    def proj(lo, hi):
        return _dot(hb, win_ref[:, lo:hi])

    qcn = (_rms(proj(C_QC, C_KVC)) * gq_ref[...]).astype(BF16)
    q = _dot(qcn, wqb_ref[...])
    qs = _dot(qcn, wqbs_ref[...])
    am = _tile_lanes(tab_ref[T_AM], MLA_HEADS)
    bm = _tile_lanes(tab_ref[T_BM], MLA_HEADS)
    qm_ref[0] = (q * am + qs * bm).astype(BF16)

    kvn = (_rms(proj(C_KVC, C_KPE)) * gkv_ref[...]).astype(BF16)
    kpe = proj(C_KPE, C_KPES) * tab_ref[T_CK] + proj(C_KPES, C_QG) * tab_ref[T_SK]
    kn = _dot(kvn, wkk_ref[...])
    km_ref[0] = (kn + _tile_lanes(kpe, MLA_HEADS)).astype(BF16)
    tm = x.shape[0]
    vm_ref[0] = _dot_nt(wkvt_ref[...], kvn).astype(BF16).reshape(MLA_HEADS, MLA_V, tm)

    bd = bd_ref[...]
    qg = proj(C_QG, C_QGS)
    qgs = proj(C_QGS, C_KG)
    rq = lax.rsqrt(_group_mean_sq(qg, bd) + NORM_EPS)
    aq = _tile_lanes(tab_ref[T_AQ], 4)
    bq = _tile_lanes(tab_ref[T_BQ], 4)
    qg_ref[0] = (rq * (qg * aq + qgs * bq)).astype(BF16)

    kg = proj(C_KG, C_KGS)
    kgs = proj(C_KGS, C_GATE)
    rk = lax.rsqrt(_group_mean_sq(kg, bd[:256, :256]) + NORM_EPS)
    ak = _tile_lanes(tab_ref[T_AK], 2)
    bk = _tile_lanes(tab_ref[T_BK], 2)
    kg_ref[0] = (rk * (kg * ak + kgs * bk)).astype(BF16)
    vg_ref[0] = _dot_nt(wvgt_ref[...], hb).astype(BF16).reshape(
        GQA_KV_HEADS, GQA_HEAD_DIM, tm)

    gate_ref[0] = jax.nn.sigmoid(proj(C_GATE, C_END)).astype(BF16)


def _inproj_call(xs, mod, g1, lw, tabs, *, n_ctx, tm):
    b, t, d = xs.shape
    nlt = (t - n_ctx) // tm
    ctx_row = b

    def tok(i, bb):
        return (bb, i, 0)

    def modmap(i, bb):
        return (jnp.where(i >= nlt, ctx_row, bb), 0, 0)

    def tok_spec(w):
        return pl.BlockSpec((1, tm, w), tok), jax.ShapeDtypeStruct((b, t, w), BF16)

    def vt_spec(heads):
        return (pl.BlockSpec((1, heads, 64, tm), lambda i, bb: (bb, 0, 0, i)),
                jax.ShapeDtypeStruct((b, heads, 64, t), BF16))

    outs = [tok_spec(1024), tok_spec(1024), vt_spec(MLA_HEADS),
            tok_spec(512), tok_spec(256), vt_spec(GQA_KV_HEADS), tok_spec(2048)]
    return pl.pallas_call(
        _inproj_kernel,
        grid=(t // tm, b),
        in_specs=[
            pl.BlockSpec((1, tm, d), tok),
            pl.BlockSpec((1, 6, d), modmap),
            _const_spec((1, d)),
            _const_spec((d, C_END)),
            _const_spec((1, MLA_Q_LORA)),
            _const_spec((MLA_Q_LORA, 1024)),
            _const_spec((MLA_Q_LORA, 1024)),
            _const_spec((1, MLA_KV_LORA)),
            _const_spec((MLA_KV_LORA, 1024)),
            _const_spec((MLA_HEADS * MLA_V, MLA_KV_LORA)),
            _const_spec((GQA_KV_HEADS * GQA_HEAD_DIM, d)),
            _const_spec((512, 512)),
            pl.BlockSpec((8, tm, LANE), lambda i, bb: (0, i, 0)),
        ],
        out_specs=[o[0] for o in outs],
        out_shape=[o[1] for o in outs],
        compiler_params=_cparams(("parallel", "arbitrary")),
        name="inproj",
    )(xs, mod, g1, lw["w_in"], lw["gq"], lw["wqb"], lw["wqbs"], lw["gkv"],
      lw["wkk"], lw["wkvt"], lw["wvgt"], lw["bd"], tabs)


KEY_BLOCK = 256


def _key_block(nkeys):
    assert nkeys % KEY_BLOCK == 0, nkeys
    return KEY_BLOCK


def _mask_gqa_q(q):
    lane = lax.broadcasted_iota(jnp.int32, q.shape, 1)
    keep = (lane // 64) == (pl.program_id(1) % 2)
    return jnp.where(keep, q, jnp.zeros_like(q))


def _attn_scores(q, k_ref, s_ref, m_ref, nkeys, probs=None, zero_ref=None):
    kb = _key_block(nkeys)
    mrun = lrun = None
    if probs is not None:
        s2_ref, p2_ref, m2_ref, l2_ref = probs
        m2 = m2_ref[0:1, :]
        zero = zero_ref[0:1, :]
    for lo in range(0, nkeys, kb):
        k = k_ref[0, lo:lo + kb, :]
        if probs is not None:
            p = jnp.exp2(s2_ref[lo:lo + kb, :] - m2)
            p2_ref[lo:lo + kb, :] = p.astype(BF16)
            f = jnp.sum(p, axis=0, keepdims=True)
            lrun = f if lrun is None else lrun + f
            tie = pltpu.bitcast(f, jnp.uint32) & zero
            tie = tie[:, :LANE] | tie[:, LANE:]
            k = pltpu.bitcast(pltpu.bitcast(k, jnp.uint32) | tie, BF16)
        s = lax.dot_general(k, q, (((1,), (1,)), ((), ())), preferred_element_type=F32)
        s_ref[lo:lo + kb, :] = s
        f = jnp.max(s, axis=0, keepdims=True)
        mrun = f if mrun is None else jnp.maximum(mrun, f)
    m_ref[...] = jnp.broadcast_to(mrun, m_ref.shape)
    if probs is not None:
        l2_ref[...] = jnp.broadcast_to(lrun, l2_ref.shape)


def _attn_probs(s_ref, p_ref, m_ref, l_ref, nkeys):
    kb = _key_block(nkeys)
    m = m_ref[0:1, :]
    lrun = None
    for lo in range(0, nkeys, kb):
        p = jnp.exp2(s_ref[lo:lo + kb, :] - m)
        f = jnp.sum(p, axis=0, keepdims=True)
        lrun = f if lrun is None else lrun + f
        p_ref[lo:lo + kb, :] = p.astype(BF16)
    l_ref[...] = jnp.broadcast_to(lrun, l_ref.shape)


def _attn_values(p_ref, vt_ref, l_ref, nkeys):
    kb = _key_block(nkeys)
    acc = None
    for lo in range(0, nkeys, kb):
        part = _dot(vt_ref[0, 0, :, lo:lo + kb], p_ref[lo:lo + kb, :])
        acc = part if acc is None else acc + part
    o_t = acc / l_ref[0:1, :]
    o_t = jnp.concatenate([o_t, jnp.zeros_like(o_t)], axis=0)
    return o_t.T.astype(BF16)


def _attn_ctx_kernel(q_ref, k_ref, vt_ref, o_ref, s_scr, p_scr, m_scr, l_scr, *, gqa):
    q = q_ref[0]
    if gqa:
        q = _mask_gqa_q(q)
    nkeys = k_ref.shape[1]
    _attn_scores(q, k_ref, s_scr, m_scr, nkeys)
    _attn_probs(s_scr, p_scr, m_scr, l_scr, nkeys)
    o_ref[0] = _attn_values(p_scr, vt_ref, l_scr, nkeys)


def _attn_lat_kernel(q_ref, k_ref, vt_ref, zero_ref, oin_ref, o_ref,
                     s0, s1, p0, p1, m0, m1, l0, l1, *, gqa, tq, nkeys):
    del oin_ref
    j = pl.program_id(2)
    last = pl.num_programs(2) - 1
    s_scr, p_scr, m_scr, l_scr = (s0, s1), (p0, p1), (m0, m1), (l0, l1)

    def scores(slot, tied_probs=False):
        q = q_ref[0, slot * tq:(slot + 1) * tq, :]
        if gqa:
            q = _mask_gqa_q(q)
        other = None
        if tied_probs:
            o = 1 - slot
            other = (s_scr[o], p_scr[o], m_scr[o], l_scr[o])
        _attn_scores(q, k_ref, s_scr[slot], m_scr[slot], nkeys, other, zero_ref)

    def probs(slot):
        _attn_probs(s_scr[slot], p_scr[slot], m_scr[slot], l_scr[slot], nkeys)

    def values(slot):
        o_ref[0, slot * tq:(slot + 1) * tq, :] = _attn_values(
            p_scr[slot], vt_ref, l_scr[slot], nkeys)

    @pl.when(j == 0)
    def _():
        scores(0)
        scores(1)
        probs(0)

    @pl.when((j > 0) & (j < last))
    def _():
        scores(0, tied_probs=True)
        values(0)
        values(1)
        scores(1)
        probs(0)

    @pl.when(j == last)
    def _():
        probs(1)
        values(0)
        values(1)


def _attn_call(q, k, vt, *, gqa, n_ctx, tq):
    b, t, _ = q.shape
    n_lat = t - n_ctx
    heads = GQA_HEADS if gqa else MLA_HEADS
    if gqa:
        qcol = lambda h: h // 2
        kcol = lambda h: h // GQA_GROUP
        vhead = kcol
    else:
        qcol = lambda h: h
        kcol = lambda h: h
        vhead = lambda h: h
    out_shape = jax.ShapeDtypeStruct((b, t, heads * LANE), BF16)
    sem = ("parallel", "parallel", "arbitrary")

    cblk = n_lat // n_ctx
    o_ctx = pl.pallas_call(
        functools.partial(_attn_ctx_kernel, gqa=gqa),
        grid=(b, heads),
        in_specs=[
            pl.BlockSpec((1, n_ctx, LANE), lambda bb, h: (bb, cblk, qcol(h))),
            pl.BlockSpec((1, n_ctx, LANE), lambda bb, h: (bb, cblk, kcol(h))),
            pl.BlockSpec((1, 1, 64, n_ctx), lambda bb, h: (bb, vhead(h), 0, cblk)),
        ],
        out_specs=pl.BlockSpec((1, n_ctx, LANE), lambda bb, h: (bb, cblk, h)),
        out_shape=out_shape,
        scratch_shapes=[pltpu.VMEM((n_ctx, n_ctx), F32), pltpu.VMEM((n_ctx, n_ctx), BF16),
                        pltpu.VMEM((8, n_ctx), F32), pltpu.VMEM((8, n_ctx), F32)],
        compiler_params=_cparams(sem[:2]),
        name="attn_ctx_gqa" if gqa else "attn_ctx_mla",
    )(q, k, vt)

    npairs = n_lat // (2 * tq)
    kern = functools.partial(_attn_lat_kernel, gqa=gqa, tq=tq, nkeys=t)
    return pl.pallas_call(
        kern,
        grid=(b, heads, npairs + 1),
        in_specs=[
            pl.BlockSpec((1, 2 * tq, LANE),
                         lambda bb, h, j: (bb, jnp.minimum(j, npairs - 1), qcol(h))),
            pl.BlockSpec((1, t, LANE), lambda bb, h, j: (bb, 0, kcol(h))),
            pl.BlockSpec((1, 1, 64, t), lambda bb, h, j: (bb, vhead(h), 0, 0)),
            pl.BlockSpec((8, tq), lambda bb, h, j: (0, 0)),
            pl.BlockSpec(memory_space=pl.ANY),
        ],
        out_specs=pl.BlockSpec((1, 2 * tq, LANE),
                               lambda bb, h, j: (bb, jnp.maximum(j - 1, 0), h)),
        out_shape=out_shape,
        scratch_shapes=[pltpu.VMEM((t, tq), F32)] * 2 + [pltpu.VMEM((t, tq), BF16)] * 2
        + [pltpu.VMEM((8, tq), F32)] * 4,
        input_output_aliases={4: 0},
        compiler_params=_cparams(sem),
        name="attn_gqa" if gqa else "attn_mla",
    )(q, k, vt, jnp.zeros((8, tq), jnp.uint32), o_ctx)


def _top2_combine(logits):
    lane = lax.broadcasted_iota(jnp.int32, logits.shape, 1)
    neg = jnp.float32(-jnp.inf)
    lg = jnp.where(lane < N_EXPERTS, logits, neg)
    m1 = jnp.max(lg, axis=-1, keepdims=True)
    i1 = jnp.min(jnp.where(lg == m1, lane, LANE), axis=-1, keepdims=True)
    sel1 = lane == i1
    lg2 = jnp.where(sel1, neg, lg)
    m2 = jnp.max(lg2, axis=-1, keepdims=True)
    i2 = jnp.min(jnp.where(lg2 == m2, lane, LANE), axis=-1, keepdims=True)
    sel2 = lane == i2
    e2 = jnp.exp(m2 - m1)
    w1 = 1.0 / (1.0 + e2)
    w2 = e2 / (1.0 + e2)
    return jnp.where(sel1, w1, 0.0) + jnp.where(sel2, w2, 0.0)


def _post_kernel(*refs, moe):
    if moe:
        (om_ref, og_ref, gate_ref, x_ref, mod_ref, wom_ref, wog_ref, wout_ref, g2_ref,
         rhi_ref, rlo_ref, xo_ref, h2_ref, comb_ref) = refs
    else:
        (om_ref, og_ref, gate_ref, x_ref, mod_ref, wom_ref, wog_ref, wout_ref, g2_ref,
         xo_ref, h2_ref) = refs
    a = _dot(om_ref[0], wom_ref[...])
    bb = _dot(og_ref[0], wog_ref[...])
    gate = gate_ref[0]
    merged = gate[:, :1024].astype(F32) * a + gate[:, 1024:].astype(F32) * bb
    mix = _dot(merged.astype(BF16), wout_ref[...])
    mod = mod_ref[0]
    x1 = x_ref[0] + mod[2:3] * mix
    xo_ref[0] = x1
    h2 = (_rms(x1) * g2_ref[...]) * (1.0 + mod[4:5]) + mod[3:4]
    h2_ref[0] = h2.astype(BF16)
    if moe:
        h_hi = h2.astype(BF16)
        h_lo = (h2 - h_hi.astype(F32)).astype(BF16)
        logits = (_dot(h_hi, rhi_ref[...]) + _dot(h_lo, rhi_ref[...])
                  + _dot(h_hi, rlo_ref[...]))
        comb_ref[0] = _top2_combine(logits)


def _post_call(om, og, gates, xs, mod, lw, *, n_ctx, tm, moe):
    b, t, d = xs.shape
    nlt = (t - n_ctx) // tm
    ctx_row = b

    def tok(i, bb):
        return (bb, i, 0)

    def modmap(i, bb):
        return (jnp.where(i >= nlt, ctx_row, bb), 0, 0)

    in_specs = [
        pl.BlockSpec((1, tm, MLA_HEADS * LANE), tok),
        pl.BlockSpec((1, tm, GQA_HEADS * LANE), tok),
        pl.BlockSpec((1, tm, 2048), tok),
        pl.BlockSpec((1, tm, d), tok),
        pl.BlockSpec((1, 6, d), modmap),
        _const_spec((MLA_HEADS * LANE, d)),
        _const_spec((GQA_HEADS * LANE, d)),
        _const_spec((d, d)),
        _const_spec((1, d)),
    ]
    args = [om, og, gates, xs, mod, lw["wom"], lw["wog"], lw["wout"], lw["g2"]]
    out_specs = [pl.BlockSpec((1, tm, d), tok), pl.BlockSpec((1, tm, d), tok)]
    out_shape = [jax.ShapeDtypeStruct((b, t, d), F32), jax.ShapeDtypeStruct((b, t, d), BF16)]
    if moe:
        in_specs += [_const_spec((d, LANE)), _const_spec((d, LANE))]
        args += [lw["r_hi"], lw["r_lo"]]
        out_specs.append(pl.BlockSpec((1, tm, LANE), tok))
        out_shape.append(jax.ShapeDtypeStruct((b, t, LANE), F32))
    return pl.pallas_call(
        functools.partial(_post_kernel, moe=moe),
        grid=(t // tm, b),
        in_specs=in_specs,
        out_specs=out_specs,
        out_shape=out_shape,
        input_output_aliases={3: 0},
        compiler_params=_cparams(("parallel", "arbitrary")),
        name="post_moe" if moe else "post_dense",
    )(*args)


def _swiglu_chunk(hb, w1, w3, w2):
    a = _dot(hb, w1)
    g = (a * jax.nn.sigmoid(a)) * _dot(hb, w3)
    return _dot(g.astype(BF16), w2)


def _dense_ffn_kernel(h_ref, x_ref, mod_ref, w1_ref, w3_ref, w2_ref, o_ref, *, fc):
    hb = h_ref[0]
    f = w1_ref.shape[1]
    acc = None
    for lo in range(0, f, fc):
        part = _swiglu_chunk(hb, w1_ref[:, lo:lo + fc], w3_ref[:, lo:lo + fc],
                             w2_ref[lo:lo + fc, :])
        acc = part if acc is None else acc + part
    o_ref[0] = x_ref[0] + mod_ref[0][5:6] * acc


def _dense_ffn_call(h2, xs, mod, lw, *, n_ctx, tm):
    b, t, d = xs.shape
    f = lw["w1"].shape[1]
    nlt = (t - n_ctx) // tm
    ctx_row = b

    def tok(i, bb):
        return (bb, i, 0)

    def modmap(i, bb):
        return (jnp.where(i >= nlt, ctx_row, bb), 0, 0)

    return pl.pallas_call(
        functools.partial(_dense_ffn_kernel, fc=f // 2),
        grid=(t // tm, b),
        in_specs=[
            pl.BlockSpec((1, tm, d), tok),
            pl.BlockSpec((1, tm, d), tok),
            pl.BlockSpec((1, 6, d), modmap),
            pl.BlockSpec((d, f), lambda i, bb: (0, 0), pipeline_mode=pl.Buffered(1)),
            pl.BlockSpec((d, f), lambda i, bb: (0, 0), pipeline_mode=pl.Buffered(1)),
            pl.BlockSpec((f, d), lambda i, bb: (0, 0), pipeline_mode=pl.Buffered(1)),
        ],
        out_specs=pl.BlockSpec((1, tm, d), tok),
        out_shape=jax.ShapeDtypeStruct((b, t, d), F32),
        input_output_aliases={1: 0},
        compiler_params=_cparams(("parallel", "arbitrary")),
        name="ffn_dense",
    )(h2, xs, mod, lw["w1"], lw["w3"], lw["w2"])


MOE_F_SPLIT = 2


def _moe_tile(n):
    for tb in (1024, 768, 512, 256):
        if n % tb == 0:
            return tb
    raise ValueError(n)


def _moe_rows(tb):
    half = tb // N_EXPERTS
    return -(-(half + half // 8) // 16) * 16


def _moe_ffn_kernel(h_ref, comb_ref, tri_ref, w1_ref, w3_ref, w2_ref, o_ref,
                    sel_scr, pos_scr, xg_scr, y_scr, *, rows):
    e = pl.program_id(1)
    fh = pl.program_id(2)
    tb = h_ref.shape[0]

    @pl.when((e == 0) & (fh == 0))
    def _():
        sel_t = (comb_ref[...].T[:16] > 0.0).astype(BF16)
        sel_scr[...] = sel_t.astype(F32)
        pos_scr[...] = lax.dot_general(sel_t, tri_ref[...], (((1,), (1,)), ((), ())),
                                       preferred_element_type=F32)
        o_ref[...] = jnp.zeros_like(o_ref)

    sel_e = sel_scr[pl.ds(e, 1), :]
    rank_e = jnp.where(sel_e > 0.0, pos_scr[pl.ds(e, 1), :], -1.0)
    count = jnp.sum(sel_e).astype(jnp.int32)
    nsub = (count + rows - 1) // rows

    def one_hot(s):
        j = lax.broadcasted_iota(jnp.int32, (rows, tb), 0) + s * rows
        return jnp.where(rank_e == j.astype(F32), 1.0, 0.0).astype(BF16)

    def block(s):
        return pl.ds(pl.multiple_of(s * rows, 16), rows)

    @pl.when(fh == 0)
    def _():
        def gather(s, carry):
            xg_scr[block(s), :] = _dot(one_hot(s), h_ref[...]).astype(BF16)
            return carry
        lax.fori_loop(0, nsub, gather, 0)

    def expert(s, carry):
        part = _swiglu_chunk(xg_scr[block(s), :], w1_ref[0], w3_ref[0], w2_ref[0])

        @pl.when(fh == 0)
        def _():
            y_scr[block(s), :] = part

        @pl.when(fh != 0)
        def _():
            y_scr[block(s), :] = y_scr[block(s), :] + part
        return carry
    lax.fori_loop(0, nsub, expert, 0)

    @pl.when(fh == pl.num_programs(2) - 1)
    def _():
        comb = comb_ref[...]
        lane = lax.broadcasted_iota(jnp.int32, comb.shape, 1)
        w_col = jnp.sum(jnp.where(lane == e, comb, 0.0), axis=-1, keepdims=True)

        def scatter(s, carry):
            z = lax.dot_general(one_hot(s), y_scr[block(s), :].astype(BF16),
                                (((0,), (0,)), ((), ())), preferred_element_type=F32)
            o_ref[...] += w_col * z
            return carry
        lax.fori_loop(0, nsub, scatter, 0)


def _moe_ffn_call(h2, comb, lw):
    n, d = h2.shape
    ne, _, f = lw["w1"].shape
    fc = f // MOE_F_SPLIT
    tb = _moe_tile(n)
    rows = _moe_rows(tb)
    nsub_max = -(-tb // rows)
    tri = (lax.broadcasted_iota(jnp.int32, (tb, tb), 1)
           < lax.broadcasted_iota(jnp.int32, (tb, tb), 0)).astype(BF16)
    return pl.pallas_call(
        functools.partial(_moe_ffn_kernel, rows=rows),
        grid=(n // tb, ne, MOE_F_SPLIT),
        in_specs=[
            pl.BlockSpec((tb, d), lambda i, e, fh: (i, 0)),
            pl.BlockSpec((tb, LANE), lambda i, e, fh: (i, 0)),
            pl.BlockSpec((tb, tb), lambda i, e, fh: (0, 0), pipeline_mode=pl.Buffered(1)),
            pl.BlockSpec((1, d, fc), lambda i, e, fh: (e, 0, fh)),
            pl.BlockSpec((1, d, fc), lambda i, e, fh: (e, 0, fh)),
            pl.BlockSpec((1, fc, d), lambda i, e, fh: (e, fh, 0)),
        ],
        out_specs=pl.BlockSpec((tb, d), lambda i, e, fh: (i, 0)),
        out_shape=jax.ShapeDtypeStruct((n, d), F32),
        scratch_shapes=[pltpu.VMEM((16, tb), F32), pltpu.VMEM((16, tb), F32),
                        pltpu.VMEM((nsub_max * rows, d), BF16),
                        pltpu.VMEM((nsub_max * rows, d), F32)],
        compiler_params=_cparams(("parallel", "arbitrary", "arbitrary")),
        name="ffn_moe",
    )(h2, comb, tri, lw["w1"], lw["w3"], lw["w2"])


def _residual_kernel(x_ref, f_ref, mod_ref, o_ref):
    o_ref[0] = x_ref[0] + mod_ref[0][5:6] * f_ref[0]


def _residual_call(xs, f, mod, *, n_ctx, tm):
    b, t, d = xs.shape
    nlt = (t - n_ctx) // tm
    ctx_row = b

    def tok(i, bb):
        return (bb, i, 0)

    def modmap(i, bb):
        return (jnp.where(i >= nlt, ctx_row, bb), 0, 0)

    return pl.pallas_call(
        _residual_kernel,
        grid=(t // tm, b),
        in_specs=[pl.BlockSpec((1, tm, d), tok), pl.BlockSpec((1, tm, d), tok),
                  pl.BlockSpec((1, 6, d), modmap)],
        out_specs=pl.BlockSpec((1, tm, d), tok),
        out_shape=jax.ShapeDtypeStruct((b, t, d), F32),
        input_output_aliases={0: 0},
        compiler_params=_cparams(("parallel", "arbitrary")),
        name="moe_residual",
    )(xs, f, mod)


def _final_norm_kernel(x_ref, g_ref, o_ref):
    o_ref[0] = _rms(x_ref[0]) * g_ref[...]


def _final_norm_res_kernel(x_ref, g_ref, f_ref, mod_ref, o_ref):
    o_ref[0] = _rms(x_ref[0] + mod_ref[0][5:6] * f_ref[0]) * g_ref[...]


def _final_norm_call(xs, g, *, n_ctx, tm, f=None, mod=None):
    b, t, d = xs.shape
    n_lat = t - n_ctx
    tok = pl.BlockSpec((1, tm, d), lambda i, bb: (bb, i, 0))
    in_specs = [tok, _const_spec((1, d))]
    args = [xs, g]
    kern = _final_norm_kernel
    if f is not None:
        in_specs += [tok, pl.BlockSpec((1, 6, d), lambda i, bb: (bb, 0, 0))]
        args += [f, mod]
        kern = _final_norm_res_kernel
    return pl.pallas_call(
        kern,
        grid=(n_lat // tm, b),
        in_specs=in_specs,
        out_specs=tok,
        out_shape=jax.ShapeDtypeStruct((b, n_lat, d), F32),
        compiler_params=_cparams(("parallel", "arbitrary")),
        name="final_norm",
    )(*args)


def _rope_parts(n_ctx, n_lat, dim):
    half = dim // 2
    quarter = half // 2
    inv = ROPE_THETA ** (-(jnp.arange(quarter, dtype=F32) * 2.0) / half)
    tpos = jnp.arange(n_lat)
    row = (tpos // GRID_W).astype(F32)
    col = (tpos % GRID_W).astype(F32)
    zeros = jnp.zeros((n_ctx, quarter), F32)
    ang_r = jnp.concatenate([row[:, None] * inv, zeros], axis=0)
    ang_c = jnp.concatenate([col[:, None] * inv, zeros], axis=0)
    cr, sr, cc, sc = jnp.cos(ang_r), jnp.sin(ang_r), jnp.cos(ang_c), jnp.sin(ang_c)
    cos = jnp.concatenate([cr, cr, cc, cc], axis=1)
    sin = jnp.concatenate([-sr, sr, -sc, sc], axis=1)
    return cos, sin


def _swap_perm(dim):
    q = dim // 4
    idx = np.arange(dim)
    return np.concatenate([idx[q:2 * q], idx[:q], idx[3 * q:], idx[2 * q:3 * q]])


def _layer_tables(cos_m, sin_m, cos_g, sin_g, gq, gk):
    t = cos_m.shape[0]
    z64 = jnp.zeros((t, 64), F32)
    z32 = jnp.zeros((t, 32), F32)
    sm = MLA_SCALE * LOG2E
    sg = GQA_SCALE * LOG2E
    pg = _swap_perm(GQA_HEAD_DIM)
    am = jnp.concatenate([jnp.full((t, 64), sm, F32), sm * cos_m, z32], axis=1)
    bm = jnp.concatenate([z64, sm * sin_m, z32], axis=1)
    ck = jnp.concatenate([z64, cos_m, z32], axis=1)
    sk = jnp.concatenate([z64, sin_m, z32], axis=1)
    aq = sg * cos_g * gq[None, :]
    bq = sg * sin_g * gq[pg][None, :]
    ak = cos_g * gk[None, :]
    bk = sin_g * gk[pg][None, :]
    two = lambda a: jnp.concatenate([a, a], axis=1)
    return jnp.stack([am, bm, ck, sk, two(aq), two(bq), two(ak), two(bk)], axis=0)


def _layer_weights(l, w_in, mla_w_qb, mla_w_kvb, w_o_mla, w_o_gqa, w_out):
    d = w_in.shape[1]
    wl = w_in[l]
    offs = np.cumsum([0, MLA_Q_LORA, MLA_KV_LORA, MLA_ROPE, 512, 128, 128, d, d])
    qc, kvc, kpe, qg, kg, vg, gm, gg = [wl[:, offs[i]:offs[i + 1]] for i in range(8)]
    pm = _swap_perm(MLA_ROPE)
    pg = _swap_perm(GQA_HEAD_DIM)
    pg8 = np.concatenate([pg + 64 * h for h in range(GQA_HEADS)])
    pg2 = np.concatenate([pg + 64 * h for h in range(GQA_KV_HEADS)])

    def pad_kpe(w):
        return jnp.pad(w, ((0, 0), (64, 32)))

    def dup(w):
        return jnp.concatenate([w[:, :64], w[:, :64], w[:, 64:], w[:, 64:]], axis=1)

    w_ext = jnp.concatenate(
        [qc, kvc, pad_kpe(kpe), pad_kpe(kpe[:, pm]), qg, qg[:, pg8],
         dup(kg), dup(kg[:, pg2]), gm, gg], axis=1).astype(BF16)

    wqb = mla_w_qb[l].reshape(MLA_Q_LORA, MLA_HEADS, MLA_NOPE + MLA_ROPE)
    rope_sw = wqb[:, :, MLA_NOPE:][:, :, pm]
    wqb_pad = jnp.pad(wqb, ((0, 0), (0, 0), (0, 32))).reshape(MLA_Q_LORA, 1024)
    wqbs_pad = jnp.pad(rope_sw, ((0, 0), (0, 0), (64, 32))).reshape(MLA_Q_LORA, 1024)

    wkvb = mla_w_kvb[l].reshape(MLA_KV_LORA, MLA_HEADS, MLA_NOPE + MLA_V)
    wkk = jnp.pad(wkvb[:, :, :MLA_NOPE], ((0, 0), (0, 0), (0, 64))).reshape(MLA_KV_LORA, 1024)
    wkv = wkvb[:, :, MLA_NOPE:].reshape(MLA_KV_LORA, MLA_HEADS * MLA_V)

    def pad_rows(w, heads):
        w = jnp.pad(w.reshape(heads, 64, d), ((0, 0), (0, 64), (0, 0)))
        return w.reshape(heads * LANE, d)

    wom = pad_rows(w_o_mla[l], MLA_HEADS)
    wog = pad_rows(w_o_gqa[l], GQA_HEADS)

    eye = np.kron(np.eye(8, dtype=np.float32), np.full((64, 64), 1.0 / 64, np.float32))
    return {
        "w_in": w_ext,
        "wqb": wqb_pad.astype(BF16),
        "wqbs": wqbs_pad.astype(BF16),
        "wkk": wkk.astype(BF16),
        "wkvt": wkv.T.astype(BF16),
        "wvgt": vg.T.astype(BF16),
        "bd": jnp.asarray(eye, BF16),
        "wom": wom.astype(BF16),
        "wog": wog.astype(BF16),
        "wout": w_out[l].astype(BF16),
    }


def kernel(x, c, ctx, c_ctx, w_ada, b_ada, norm1_g, norm2_g, w_in, mla_q_norm_g, mla_w_qb,
           mla_kv_norm_g, mla_w_kvb, gqa_q_norm_g, gqa_k_norm_g, w_o_mla, w_o_gqa, w_out,
           dense_w1, dense_w3, dense_w2, moe_router, moe_w1, moe_w3, moe_w2, final_norm_g):
    b, n_lat, d = x.shape
    n_ctx = ctx.shape[1]
    depth = w_ada.shape[0]
    t = n_ctx + n_lat
    tm, tq = TOKEN_TILE, QUERY_TILE
    assert n_ctx % tm == 0 and n_lat % tm == 0 and n_lat % GRID_W == 0
    assert n_lat % (2 * tq) == 0 and n_lat % n_ctx == 0 and n_ctx % KEY_BLOCK == 0

    rows = -(-(b + 1) // 8) * 8
    cc = jnp.zeros((rows, d), F32).at[:b].set(c).at[b].set(c_ctx)
    mod_all = _ada_call(cc, w_ada, b_ada).reshape(depth, rows, 6, d)

    cos_m, sin_m = _rope_parts(n_ctx, n_lat, MLA_ROPE)
    cos_g, sin_g = _rope_parts(n_ctx, n_lat, GQA_HEAD_DIM)

    xs = jnp.concatenate([x, ctx], axis=1)
    for l in range(depth):
        lw = _layer_weights(l, w_in, mla_w_qb, mla_w_kvb, w_o_mla, w_o_gqa, w_out)
        lw["gq"] = mla_q_norm_g[l][None, :]
        lw["gkv"] = mla_kv_norm_g[l][None, :]
        lw["g2"] = norm2_g[l][None, :]
        tabs = _layer_tables(cos_m, sin_m, cos_g, sin_g, gqa_q_norm_g[l], gqa_k_norm_g[l])
        mod = mod_all[l]
        moe = l % 2 == 1

        qm, km, vm, qg, kg, vg, gates = _inproj_call(
            xs, mod, norm1_g[l][None, :], lw, tabs, n_ctx=n_ctx, tm=tm)
        om = _attn_call(qm, km, vm, gqa=False, n_ctx=n_ctx, tq=tq)
        og = _attn_call(qg, kg, vg, gqa=True, n_ctx=n_ctx, tq=tq)

        if moe:
            r = jnp.pad(moe_router[l // 2], ((0, 0), (0, LANE - N_EXPERTS)))
            lw["r_hi"] = r.astype(BF16)
            lw["r_lo"] = (r - lw["r_hi"].astype(F32)).astype(BF16)
            xs, h2, comb = _post_call(om, og, gates, xs, mod, lw, n_ctx=n_ctx, tm=tm, moe=True)
            lw["w1"] = moe_w1[l // 2].astype(BF16)
            lw["w3"] = moe_w3[l // 2].astype(BF16)
            lw["w2"] = moe_w2[l // 2].astype(BF16)
            f = _moe_ffn_call(h2.reshape(b * t, d), comb.reshape(b * t, LANE), lw)
            f = f.reshape(b, t, d)
            if l == depth - 1:
                return _final_norm_call(xs, final_norm_g[None, :], n_ctx=n_ctx, tm=tm,
                                        f=f, mod=mod)
            xs = _residual_call(xs, f, mod, n_ctx=n_ctx, tm=tm)
        else:
            xs, h2 = _post_call(om, og, gates, xs, mod, lw, n_ctx=n_ctx, tm=tm, moe=False)
            lw["w1"] = dense_w1[l // 2].astype(BF16)
            lw["w3"] = dense_w3[l // 2].astype(BF16)
            lw["w2"] = dense_w2[l // 2].astype(BF16)
            xs = _dense_ffn_call(h2, xs, mod, lw, n_ctx=n_ctx, tm=tm)

    return _final_norm_call(xs, final_norm_g[None, :], n_ctx=n_ctx, tm=tm)
```

```python
import functools
import math

import numpy as np
import jax
import jax.numpy as jnp
from jax import lax
from jax.experimental import pallas as pl
from jax.experimental.pallas import tpu as pltpu

F32 = jnp.float32
BF16 = jnp.bfloat16

GRID_W = 64
ROPE_THETA = 10000.0
NORM_EPS = 1e-6
MLA_HEADS = 8
MLA_Q_LORA = 384
MLA_KV_LORA = 256
MLA_NOPE = 64
MLA_ROPE = 32
MLA_V = 64
MLA_SCALE = (MLA_NOPE + MLA_ROPE) ** -0.5
GQA_HEADS = 8
GQA_KV_HEADS = 2
GQA_GROUP = GQA_HEADS // GQA_KV_HEADS
GQA_HEAD_DIM = 64
GQA_SCALE = GQA_HEAD_DIM ** -0.5
N_EXPERTS = 8
LOG2E = math.log2(math.e)

LANE = 128
VMEM_LIMIT = 56 * 1024 * 1024
TOKEN_TILE = 256
QUERY_TILE = 256

C_QC = 0
C_KVC = C_QC + MLA_Q_LORA
C_KPE = C_KVC + MLA_KV_LORA
C_KPES = C_KPE + LANE
C_QG = C_KPES + LANE
C_QGS = C_QG + 512
C_KG = C_QGS + 512
C_KGS = C_KG + 256
C_GATE = C_KGS + 256
C_END = C_GATE + 2048

T_AM, T_BM, T_CK, T_SK, T_AQ, T_BQ, T_AK, T_BK = range(8)


def _cparams(sem):
    return pltpu.CompilerParams(dimension_semantics=sem, vmem_limit_bytes=VMEM_LIMIT)


def _const_spec(shape):
    nd = len(shape)
    return pl.BlockSpec(shape, lambda *_: (0,) * nd)


def _dot(a, b):
    return jnp.dot(a, b, preferred_element_type=F32)


def _rms(x):
    return x * lax.rsqrt(jnp.mean(x * x, axis=-1, keepdims=True) + NORM_EPS)


def _ada_kernel(cc_ref, w_ref, b_ref, o_ref):
    a = cc_ref[...]
    a = a * jax.nn.sigmoid(a)
    a_hi = a.astype(BF16)
    a_lo = (a - a_hi.astype(F32)).astype(BF16)
    w = w_ref[0]
    w_hi = w.astype(BF16)
    w_lo = (w - w_hi.astype(F32)).astype(BF16)
    acc = _dot(a_hi, w_hi) + _dot(a_lo, w_hi) + _dot(a_hi, w_lo)
    o_ref[0] = acc + b_ref[0]


def _ada_call(cc, w_ada, b_ada):
    depth, d, n = w_ada.shape
    rows = cc.shape[0]
    tn = 1536
    return pl.pallas_call(
        _ada_kernel,
        grid=(depth, n // tn),
        in_specs=[
            pl.BlockSpec((rows, d), lambda l, j: (0, 0)),
            pl.BlockSpec((1, d, tn), lambda l, j: (l, 0, j)),
            pl.BlockSpec((1, 1, tn), lambda l, j: (l, 0, j)),
        ],
        out_specs=pl.BlockSpec((1, rows, tn), lambda l, j: (l, 0, j)),
        out_shape=jax.ShapeDtypeStruct((depth, rows, n), F32),
        compiler_params=_cparams(("arbitrary", "arbitrary")),
        name="ada",
    )(cc, w_ada, b_ada.reshape(depth, 1, n))


def _group_mean_sq(x, bd):
    x2 = x * x
    hi = x2.astype(BF16)
    lo = (x2 - hi.astype(F32)).astype(BF16)
    return _dot(hi, bd) + _dot(lo, bd)


def _tile_lanes(t, n):
    return jnp.concatenate([t] * n, axis=1)


def _dot_nt(a, b):
    return lax.dot_general(a, b, (((1,), (1,)), ((), ())), preferred_element_type=F32)


def _inproj_kernel(x_ref, mod_ref, g1_ref, win_ref, gq_ref, wqb_ref, wqbs_ref, gkv_ref,
                   wkk_ref, wkvt_ref, wvgt_ref, bd_ref, tab_ref,
                   qm_ref, km_ref, vm_ref, qg_ref, kg_ref, vg_ref, gate_ref):
    x = x_ref[0]
    mod = mod_ref[0]
    h = (_rms(x) * g1_ref[...]) * (1.0 + mod[1:2]) + mod[0:1]
    hb = h.astype(BF16)

    def proj(lo, hi):
        return _dot(hb, win_ref[:, lo:hi])

    qcn = (_rms(proj(C_QC, C_KVC)) * gq_ref[...]).astype(BF16)
    q = _dot(qcn, wqb_ref[...])
    qs = _dot(qcn, wqbs_ref[...])
    am = _tile_lanes(tab_ref[T_AM], MLA_HEADS)
    bm = _tile_lanes(tab_ref[T_BM], MLA_HEADS)
    qm_ref[0] = (q * am + qs * bm).astype(BF16)

    kvn = (_rms(proj(C_KVC, C_KPE)) * gkv_ref[...]).astype(BF16)
    kpe = proj(C_KPE, C_KPES) * tab_ref[T_CK] + proj(C_KPES, C_QG) * tab_ref[T_SK]
    kn = _dot(kvn, wkk_ref[...])
    km_ref[0] = (kn + _tile_lanes(kpe, MLA_HEADS)).astype(BF16)
    tm = x.shape[0]
    vm_ref[0] = _dot_nt(wkvt_ref[...], kvn).astype(BF16).reshape(MLA_HEADS, MLA_V, tm)

    bd = bd_ref[...]
    qg = proj(C_QG, C_QGS)
    qgs = proj(C_QGS, C_KG)
    rq = lax.rsqrt(_group_mean_sq(qg, bd) + NORM_EPS)
    aq = _tile_lanes(tab_ref[T_AQ], 4)
    bq = _tile_lanes(tab_ref[T_BQ], 4)
    qg_ref[0] = (rq * (qg * aq + qgs * bq)).astype(BF16)

    kg = proj(C_KG, C_KGS)
    kgs = proj(C_KGS, C_GATE)
    rk = lax.rsqrt(_group_mean_sq(kg, bd[:256, :256]) + NORM_EPS)
    ak = _tile_lanes(tab_ref[T_AK], 2)
    bk = _tile_lanes(tab_ref[T_BK], 2)
    kg_ref[0] = (rk * (kg * ak + kgs * bk)).astype(BF16)
    vg_ref[0] = _dot_nt(wvgt_ref[...], hb).astype(BF16).reshape(
        GQA_KV_HEADS, GQA_HEAD_DIM, tm)

    gate_ref[0] = jax.nn.sigmoid(proj(C_GATE, C_END)).astype(BF16)


def _inproj_call(xs, mod, g1, lw, tabs, *, n_ctx, tm):
    b, t, d = xs.shape
    nlt = (t - n_ctx) // tm
    ctx_row = b

    def tok(i, bb):
        return (bb, i, 0)

    def modmap(i, bb):
        return (jnp.where(i >= nlt, ctx_row, bb), 0, 0)

    def tok_spec(w):
        return pl.BlockSpec((1, tm, w), tok), jax.ShapeDtypeStruct((b, t, w), BF16)

    def vt_spec(heads):
        return (pl.BlockSpec((1, heads, 64, tm), lambda i, bb: (bb, 0, 0, i)),
                jax.ShapeDtypeStruct((b, heads, 64, t), BF16))

    outs = [tok_spec(1024), tok_spec(1024), vt_spec(MLA_HEADS),
            tok_spec(512), tok_spec(256), vt_spec(GQA_KV_HEADS), tok_spec(2048)]
    return pl.pallas_call(
        _inproj_kernel,
        grid=(t // tm, b),
        in_specs=[
            pl.BlockSpec((1, tm, d), tok),
            pl.BlockSpec((1, 6, d), modmap),
            _const_spec((1, d)),
            _const_spec((d, C_END)),
            _const_spec((1, MLA_Q_LORA)),
            _const_spec((MLA_Q_LORA, 1024)),
            _const_spec((MLA_Q_LORA, 1024)),
            _const_spec((1, MLA_KV_LORA)),
            _const_spec((MLA_KV_LORA, 1024)),
            _const_spec((MLA_HEADS * MLA_V, MLA_KV_LORA)),
            _const_spec((GQA_KV_HEADS * GQA_HEAD_DIM, d)),
            _const_spec((512, 512)),
            pl.BlockSpec((8, tm, LANE), lambda i, bb: (0, i, 0)),
        ],
        out_specs=[o[0] for o in outs],
        out_shape=[o[1] for o in outs],
        compiler_params=_cparams(("parallel", "arbitrary")),
        name="inproj",
    )(xs, mod, g1, lw["w_in"], lw["gq"], lw["wqb"], lw["wqbs"], lw["gkv"],
      lw["wkk"], lw["wkvt"], lw["wvgt"], lw["bd"], tabs)


KEY_BLOCK = 256


def _key_block(nkeys):
    assert nkeys % KEY_BLOCK == 0, nkeys
    return KEY_BLOCK


def _mask_gqa_q(q):
    lane = lax.broadcasted_iota(jnp.int32, q.shape, 1)
    keep = (lane // 64) == (pl.program_id(1) % 2)
    return jnp.where(keep, q, jnp.zeros_like(q))


def _attn_scores(q, k_ref, s_ref, m_ref, nkeys, probs=None, zero_ref=None):
    kb = _key_block(nkeys)
    mrun = lrun = None
    if probs is not None:
        s2_ref, p2_ref, m2_ref, l2_ref = probs
        m2 = m2_ref[0:1, :]
        zero = zero_ref[0:1, :]
    for lo in range(0, nkeys, kb):
        k = k_ref[0, lo:lo + kb, :]
        if probs is not None:
            p = jnp.exp2(s2_ref[lo:lo + kb, :] - m2)
            p2_ref[lo:lo + kb, :] = p.astype(BF16)
            f = jnp.sum(p, axis=0, keepdims=True)
            lrun = f if lrun is None else lrun + f
            tie = pltpu.bitcast(f, jnp.uint32) & zero
            tie = tie[:, :LANE] | tie[:, LANE:]
            k = pltpu.bitcast(pltpu.bitcast(k, jnp.uint32) | tie, BF16)
        s = lax.dot_general(k, q, (((1,), (1,)), ((), ())), preferred_element_type=F32)
        s_ref[lo:lo + kb, :] = s
        f = jnp.max(s, axis=0, keepdims=True)
        mrun = f if mrun is None else jnp.maximum(mrun, f)
    m_ref[...] = jnp.broadcast_to(mrun, m_ref.shape)
    if probs is not None:
        l2_ref[...] = jnp.broadcast_to(lrun, l2_ref.shape)


def _attn_probs(s_ref, p_ref, m_ref, l_ref, nkeys):
    kb = _key_block(nkeys)
    m = m_ref[0:1, :]
    lrun = None
    for lo in range(0, nkeys, kb):
        p = jnp.exp2(s_ref[lo:lo + kb, :] - m)
        f = jnp.sum(p, axis=0, keepdims=True)
        lrun = f if lrun is None else lrun + f
        p_ref[lo:lo + kb, :] = p.astype(BF16)
    l_ref[...] = jnp.broadcast_to(lrun, l_ref.shape)


def _attn_values(p_ref, vt_ref, l_ref, nkeys):
    kb = _key_block(nkeys)
    acc = None
    for lo in range(0, nkeys, kb):
        part = _dot(vt_ref[0, 0, :, lo:lo + kb], p_ref[lo:lo + kb, :])
        acc = part if acc is None else acc + part
    o_t = acc / l_ref[0:1, :]
    o_t = jnp.concatenate([o_t, jnp.zeros_like(o_t)], axis=0)
    return o_t.T.astype(BF16)


def _attn_ctx_kernel(q_ref, k_ref, vt_ref, o_ref, s_scr, p_scr, m_scr, l_scr, *, gqa):
    q = q_ref[0]
    if gqa:
        q = _mask_gqa_q(q)
    nkeys = k_ref.shape[1]
    _attn_scores(q, k_ref, s_scr, m_scr, nkeys)
    _attn_probs(s_scr, p_scr, m_scr, l_scr, nkeys)
    o_ref[0] = _attn_values(p_scr, vt_ref, l_scr, nkeys)


def _attn_lat_kernel(q_ref, k_ref, vt_ref, zero_ref, oin_ref, o_ref,
                     s0, s1, p0, p1, m0, m1, l0, l1, *, gqa, tq, nkeys):
    del oin_ref
    j = pl.program_id(2)
    last = pl.num_programs(2) - 1
    s_scr, p_scr, m_scr, l_scr = (s0, s1), (p0, p1), (m0, m1), (l0, l1)

    def scores(slot, tied_probs=False):
        q = q_ref[0, slot * tq:(slot + 1) * tq, :]
        if gqa:
            q = _mask_gqa_q(q)
        other = None
        if tied_probs:
            o = 1 - slot
            other = (s_scr[o], p_scr[o], m_scr[o], l_scr[o])
        _attn_scores(q, k_ref, s_scr[slot], m_scr[slot], nkeys, other, zero_ref)

    def probs(slot):
        _attn_probs(s_scr[slot], p_scr[slot], m_scr[slot], l_scr[slot], nkeys)

    def values(slot):
        o_ref[0, slot * tq:(slot + 1) * tq, :] = _attn_values(
            p_scr[slot], vt_ref, l_scr[slot], nkeys)

    @pl.when(j == 0)
    def _():
        scores(0)
        scores(1)
        probs(0)

    @pl.when((j > 0) & (j < last))
    def _():
        scores(0, tied_probs=True)
        values(0)
        scores(1)
        probs(0)
        values(1)

    @pl.when(j == last)
    def _():
        probs(1)
        values(0)
        values(1)


def _attn_call(q, k, vt, *, gqa, n_ctx, tq):
    b, t, _ = q.shape
    n_lat = t - n_ctx
    heads = GQA_HEADS if gqa else MLA_HEADS
    if gqa:
        qcol = lambda h: h // 2
        kcol = lambda h: h // GQA_GROUP
        vhead = kcol
    else:
        qcol = lambda h: h
        kcol = lambda h: h
        vhead = lambda h: h
    out_shape = jax.ShapeDtypeStruct((b, t, heads * LANE), BF16)
    sem = ("parallel", "parallel", "arbitrary")

    cblk = n_lat // n_ctx
    o_ctx = pl.pallas_call(
        functools.partial(_attn_ctx_kernel, gqa=gqa),
        grid=(b, heads),
        in_specs=[
            pl.BlockSpec((1, n_ctx, LANE), lambda bb, h: (bb, cblk, qcol(h))),
            pl.BlockSpec((1, n_ctx, LANE), lambda bb, h: (bb, cblk, kcol(h))),
            pl.BlockSpec((1, 1, 64, n_ctx), lambda bb, h: (bb, vhead(h), 0, cblk)),
        ],
        out_specs=pl.BlockSpec((1, n_ctx, LANE), lambda bb, h: (bb, cblk, h)),
        out_shape=out_shape,
        scratch_shapes=[pltpu.VMEM((n_ctx, n_ctx), F32), pltpu.VMEM((n_ctx, n_ctx), BF16),
                        pltpu.VMEM((8, n_ctx), F32), pltpu.VMEM((8, n_ctx), F32)],
        compiler_params=_cparams(sem[:2]),
        name="attn_ctx_gqa" if gqa else "attn_ctx_mla",
    )(q, k, vt)

    npairs = n_lat // (2 * tq)
    kern = functools.partial(_attn_lat_kernel, gqa=gqa, tq=tq, nkeys=t)
    return pl.pallas_call(
        kern,
        grid=(b, heads, npairs + 1),
        in_specs=[
            pl.BlockSpec((1, 2 * tq, LANE),
                         lambda bb, h, j: (bb, jnp.minimum(j, npairs - 1), qcol(h))),
            pl.BlockSpec((1, t, LANE), lambda bb, h, j: (bb, 0, kcol(h))),
            pl.BlockSpec((1, 1, 64, t), lambda bb, h, j: (bb, vhead(h), 0, 0)),
            pl.BlockSpec((8, tq), lambda bb, h, j: (0, 0)),
            pl.BlockSpec(memory_space=pl.ANY),
        ],
        out_specs=pl.BlockSpec((1, 2 * tq, LANE),
                               lambda bb, h, j: (bb, jnp.maximum(j - 1, 0), h)),
        out_shape=out_shape,
        scratch_shapes=[pltpu.VMEM((t, tq), F32)] * 2 + [pltpu.VMEM((t, tq), BF16)] * 2
        + [pltpu.VMEM((8, tq), F32)] * 4,
        input_output_aliases={4: 0},
        compiler_params=_cparams(sem),
        name="attn_gqa" if gqa else "attn_mla",
    )(q, k, vt, jnp.zeros((8, tq), jnp.uint32), o_ctx)


def _top2_combine(logits):
    lane = lax.broadcasted_iota(jnp.int32, logits.shape, 1)
    neg = jnp.float32(-jnp.inf)
    lg = jnp.where(lane < N_EXPERTS, logits, neg)
    m1 = jnp.max(lg, axis=-1, keepdims=True)
    i1 = jnp.min(jnp.where(lg == m1, lane, LANE), axis=-1, keepdims=True)
    sel1 = lane == i1
    lg2 = jnp.where(sel1, neg, lg)
    m2 = jnp.max(lg2, axis=-1, keepdims=True)
    i2 = jnp.min(jnp.where(lg2 == m2, lane, LANE), axis=-1, keepdims=True)
    sel2 = lane == i2
    e2 = jnp.exp(m2 - m1)
    w1 = 1.0 / (1.0 + e2)
    w2 = e2 / (1.0 + e2)
    return jnp.where(sel1, w1, 0.0) + jnp.where(sel2, w2, 0.0)


def _post_kernel(*refs, moe):
    if moe:
        (om_ref, og_ref, gate_ref, x_ref, mod_ref, wom_ref, wog_ref, wout_ref, g2_ref,
         rhi_ref, rlo_ref, xo_ref, h2_ref, comb_ref) = refs
    else:
        (om_ref, og_ref, gate_ref, x_ref, mod_ref, wom_ref, wog_ref, wout_ref, g2_ref,
         xo_ref, h2_ref) = refs
    a = _dot(om_ref[0], wom_ref[...])
    bb = _dot(og_ref[0], wog_ref[...])
    gate = gate_ref[0]
    merged = gate[:, :1024].astype(F32) * a + gate[:, 1024:].astype(F32) * bb
    mix = _dot(merged.astype(BF16), wout_ref[...])
    mod = mod_ref[0]
    x1 = x_ref[0] + mod[2:3] * mix
    xo_ref[0] = x1
    h2 = (_rms(x1) * g2_ref[...]) * (1.0 + mod[4:5]) + mod[3:4]
    h2_ref[0] = h2.astype(BF16)
    if moe:
        h_hi = h2.astype(BF16)
        h_lo = (h2 - h_hi.astype(F32)).astype(BF16)
        logits = (_dot(h_hi, rhi_ref[...]) + _dot(h_lo, rhi_ref[...])
                  + _dot(h_hi, rlo_ref[...]))
        comb_ref[0] = _top2_combine(logits)


def _post_call(om, og, gates, xs, mod, lw, *, n_ctx, tm, moe):
    b, t, d = xs.shape
    nlt = (t - n_ctx) // tm
    ctx_row = b

    def tok(i, bb):
        return (bb, i, 0)

    def modmap(i, bb):
        return (jnp.where(i >= nlt, ctx_row, bb), 0, 0)

    in_specs = [
        pl.BlockSpec((1, tm, MLA_HEADS * LANE), tok),
        pl.BlockSpec((1, tm, GQA_HEADS * LANE), tok),
        pl.BlockSpec((1, tm, 2048), tok),
        pl.BlockSpec((1, tm, d), tok),
        pl.BlockSpec((1, 6, d), modmap),
        _const_spec((MLA_HEADS * LANE, d)),
        _const_spec((GQA_HEADS * LANE, d)),
        _const_spec((d, d)),
        _const_spec((1, d)),
    ]
    args = [om, og, gates, xs, mod, lw["wom"], lw["wog"], lw["wout"], lw["g2"]]
    out_specs = [pl.BlockSpec((1, tm, d), tok), pl.BlockSpec((1, tm, d), tok)]
    out_shape = [jax.ShapeDtypeStruct((b, t, d), F32), jax.ShapeDtypeStruct((b, t, d), BF16)]
    if moe:
        in_specs += [_const_spec((d, LANE)), _const_spec((d, LANE))]
        args += [lw["r_hi"], lw["r_lo"]]
        out_specs.append(pl.BlockSpec((1, tm, LANE), tok))
        out_shape.append(jax.ShapeDtypeStruct((b, t, LANE), F32))
    return pl.pallas_call(
        functools.partial(_post_kernel, moe=moe),
        grid=(t // tm, b),
        in_specs=in_specs,
        out_specs=out_specs,
        out_shape=out_shape,
        input_output_aliases={3: 0},
        compiler_params=_cparams(("parallel", "arbitrary")),
        name="post_moe" if moe else "post_dense",
    )(*args)


def _swiglu_chunk(hb, w1, w3, w2):
    a = _dot(hb, w1)
    g = (a * jax.nn.sigmoid(a)) * _dot(hb, w3)
    return _dot(g.astype(BF16), w2)


def _dense_ffn_kernel(h_ref, x_ref, mod_ref, w1_ref, w3_ref, w2_ref, o_ref):
    f = _swiglu_chunk(h_ref[0], w1_ref[...], w3_ref[...], w2_ref[...])
    o_ref[0] = x_ref[0] + mod_ref[0][5:6] * f


def _dense_ffn_call(h2, xs, mod, lw, *, n_ctx, tm):
    b, t, d = xs.shape
    f = lw["w1"].shape[1]
    nlt = (t - n_ctx) // tm
    ctx_row = b

    def tok(i, bb):
        return (bb, i, 0)

    def modmap(i, bb):
        return (jnp.where(i >= nlt, ctx_row, bb), 0, 0)

    return pl.pallas_call(
        _dense_ffn_kernel,
        grid=(t // tm, b),
        in_specs=[
            pl.BlockSpec((1, tm, d), tok),
            pl.BlockSpec((1, tm, d), tok),
            pl.BlockSpec((1, 6, d), modmap),
            pl.BlockSpec((d, f), lambda i, bb: (0, 0), pipeline_mode=pl.Buffered(1)),
            pl.BlockSpec((d, f), lambda i, bb: (0, 0), pipeline_mode=pl.Buffered(1)),
            pl.BlockSpec((f, d), lambda i, bb: (0, 0), pipeline_mode=pl.Buffered(1)),
        ],
        out_specs=pl.BlockSpec((1, tm, d), tok),
        out_shape=jax.ShapeDtypeStruct((b, t, d), F32),
        input_output_aliases={1: 0},
        compiler_params=_cparams(("parallel", "arbitrary")),
        name="ffn_dense",
    )(h2, xs, mod, lw["w1"], lw["w3"], lw["w2"])


MOE_F_SPLIT = 2


def _moe_tile(n):
    for tb in (1024, 768, 512, 256):
        if n % tb == 0:
            return tb
    raise ValueError(n)


def _moe_rows(tb):
    half = tb // N_EXPERTS
    return -(-(half + half // 8) // 16) * 16


def _moe_ffn_kernel(h_ref, comb_ref, tri_ref, w1_ref, w3_ref, w2_ref, o_ref,
                    sel_scr, pos_scr, xg_scr, y_scr, *, rows):
    e = pl.program_id(1)
    fh = pl.program_id(2)
    tb = h_ref.shape[0]

    @pl.when((e == 0) & (fh == 0))
    def _():
        sel_t = (comb_ref[...].T[:16] > 0.0).astype(BF16)
        sel_scr[...] = sel_t.astype(F32)
        pos_scr[...] = lax.dot_general(sel_t, tri_ref[...], (((1,), (1,)), ((), ())),
                                       preferred_element_type=F32)
        o_ref[...] = jnp.zeros_like(o_ref)

    sel_e = sel_scr[pl.ds(e, 1), :]
    rank_e = jnp.where(sel_e > 0.0, pos_scr[pl.ds(e, 1), :], -1.0)
    count = jnp.sum(sel_e).astype(jnp.int32)
    nsub = (count + rows - 1) // rows

    def one_hot(s):
        j = lax.broadcasted_iota(jnp.int32, (rows, tb), 0) + s * rows
        return jnp.where(rank_e == j.astype(F32), 1.0, 0.0).astype(BF16)

    def block(s):
        return pl.ds(pl.multiple_of(s * rows, 16), rows)

    @pl.when(fh == 0)
    def _():
        def gather(s, carry):
            xg_scr[block(s), :] = _dot(one_hot(s), h_ref[...]).astype(BF16)
            return carry
        lax.fori_loop(0, nsub, gather, 0)

    def expert(s, carry):
        part = _swiglu_chunk(xg_scr[block(s), :], w1_ref[0], w3_ref[0], w2_ref[0])

        @pl.when(fh == 0)
        def _():
            y_scr[block(s), :] = part

        @pl.when(fh != 0)
        def _():
            y_scr[block(s), :] = y_scr[block(s), :] + part
        return carry
    lax.fori_loop(0, nsub, expert, 0)

    @pl.when(fh == pl.num_programs(2) - 1)
    def _():
        comb = comb_ref[...]
        lane = lax.broadcasted_iota(jnp.int32, comb.shape, 1)
        w_col = jnp.sum(jnp.where(lane == e, comb, 0.0), axis=-1, keepdims=True)

        def scatter(s, carry):
            z = lax.dot_general(one_hot(s), y_scr[block(s), :].astype(BF16),
                                (((0,), (0,)), ((), ())), preferred_element_type=F32)
            o_ref[...] += w_col * z
            return carry
        lax.fori_loop(0, nsub, scatter, 0)


def _moe_ffn_call(h2, comb, lw):
    n, d = h2.shape
    ne, _, f = lw["w1"].shape
    fc = f // MOE_F_SPLIT
    tb = _moe_tile(n)
    rows = _moe_rows(tb)
    nsub_max = -(-tb // rows)
    tri = (lax.broadcasted_iota(jnp.int32, (tb, tb), 1)
           < lax.broadcasted_iota(jnp.int32, (tb, tb), 0)).astype(BF16)
    return pl.pallas_call(
        functools.partial(_moe_ffn_kernel, rows=rows),
        grid=(n // tb, ne, MOE_F_SPLIT),
        in_specs=[
            pl.BlockSpec((tb, d), lambda i, e, fh: (i, 0)),
            pl.BlockSpec((tb, LANE), lambda i, e, fh: (i, 0)),
            pl.BlockSpec((tb, tb), lambda i, e, fh: (0, 0), pipeline_mode=pl.Buffered(1)),
            pl.BlockSpec((1, d, fc), lambda i, e, fh: (e, 0, fh)),
            pl.BlockSpec((1, d, fc), lambda i, e, fh: (e, 0, fh)),
            pl.BlockSpec((1, fc, d), lambda i, e, fh: (e, fh, 0)),
        ],
        out_specs=pl.BlockSpec((tb, d), lambda i, e, fh: (i, 0)),
        out_shape=jax.ShapeDtypeStruct((n, d), F32),
        scratch_shapes=[pltpu.VMEM((16, tb), F32), pltpu.VMEM((16, tb), F32),
                        pltpu.VMEM((nsub_max * rows, d), BF16),
                        pltpu.VMEM((nsub_max * rows, d), F32)],
        compiler_params=_cparams(("parallel", "arbitrary", "arbitrary")),
        name="ffn_moe",
    )(h2, comb, tri, lw["w1"], lw["w3"], lw["w2"])


def _residual_kernel(x_ref, f_ref, mod_ref, o_ref):
    o_ref[0] = x_ref[0] + mod_ref[0][5:6] * f_ref[0]


def _residual_call(xs, f, mod, *, n_ctx, tm):
    b, t, d = xs.shape
    nlt = (t - n_ctx) // tm
    ctx_row = b

    def tok(i, bb):
        return (bb, i, 0)

    def modmap(i, bb):
        return (jnp.where(i >= nlt, ctx_row, bb), 0, 0)

    return pl.pallas_call(
        _residual_kernel,
        grid=(t // tm, b),
        in_specs=[pl.BlockSpec((1, tm, d), tok), pl.BlockSpec((1, tm, d), tok),
                  pl.BlockSpec((1, 6, d), modmap)],
        out_specs=pl.BlockSpec((1, tm, d), tok),
        out_shape=jax.ShapeDtypeStruct((b, t, d), F32),
        input_output_aliases={0: 0},
        compiler_params=_cparams(("parallel", "arbitrary")),
        name="moe_residual",
    )(xs, f, mod)


def _final_norm_kernel(x_ref, g_ref, o_ref):
    o_ref[0] = _rms(x_ref[0]) * g_ref[...]


def _final_norm_res_kernel(x_ref, g_ref, f_ref, mod_ref, o_ref):
    o_ref[0] = _rms(x_ref[0] + mod_ref[0][5:6] * f_ref[0]) * g_ref[...]


def _final_norm_call(xs, g, *, n_ctx, tm, f=None, mod=None):
    b, t, d = xs.shape
    n_lat = t - n_ctx
    tok = pl.BlockSpec((1, tm, d), lambda i, bb: (bb, i, 0))
    in_specs = [tok, _const_spec((1, d))]
    args = [xs, g]
    kern = _final_norm_kernel
    if f is not None:
        in_specs += [tok, pl.BlockSpec((1, 6, d), lambda i, bb: (bb, 0, 0))]
        args += [f, mod]
        kern = _final_norm_res_kernel
    return pl.pallas_call(
        kern,
        grid=(n_lat // tm, b),
        in_specs=in_specs,
        out_specs=tok,
        out_shape=jax.ShapeDtypeStruct((b, n_lat, d), F32),
        compiler_params=_cparams(("parallel", "arbitrary")),
        name="final_norm",
    )(*args)


def _rope_parts(n_ctx, n_lat, dim):
    half = dim // 2
    quarter = half // 2
    inv = ROPE_THETA ** (-(jnp.arange(quarter, dtype=F32) * 2.0) / half)
    tpos = jnp.arange(n_lat)
    row = (tpos // GRID_W).astype(F32)
    col = (tpos % GRID_W).astype(F32)
    zeros = jnp.zeros((n_ctx, quarter), F32)
    ang_r = jnp.concatenate([row[:, None] * inv, zeros], axis=0)
    ang_c = jnp.concatenate([col[:, None] * inv, zeros], axis=0)
    cr, sr, cc, sc = jnp.cos(ang_r), jnp.sin(ang_r), jnp.cos(ang_c), jnp.sin(ang_c)
    cos = jnp.concatenate([cr, cr, cc, cc], axis=1)
    sin = jnp.concatenate([-sr, sr, -sc, sc], axis=1)
    return cos, sin


def _swap_perm(dim):
    q = dim // 4
    idx = np.arange(dim)
    return np.concatenate([idx[q:2 * q], idx[:q], idx[3 * q:], idx[2 * q:3 * q]])


def _layer_tables(cos_m, sin_m, cos_g, sin_g, gq, gk):
    t = cos_m.shape[0]
    z64 = jnp.zeros((t, 64), F32)
    z32 = jnp.zeros((t, 32), F32)
    sm = MLA_SCALE * LOG2E
    sg = GQA_SCALE * LOG2E
    pg = _swap_perm(GQA_HEAD_DIM)
    am = jnp.concatenate([jnp.full((t, 64), sm, F32), sm * cos_m, z32], axis=1)
    bm = jnp.concatenate([z64, sm * sin_m, z32], axis=1)
    ck = jnp.concatenate([z64, cos_m, z32], axis=1)
    sk = jnp.concatenate([z64, sin_m, z32], axis=1)
    aq = sg * cos_g * gq[None, :]
    bq = sg * sin_g * gq[pg][None, :]
    ak = cos_g * gk[None, :]
    bk = sin_g * gk[pg][None, :]
    two = lambda a: jnp.concatenate([a, a], axis=1)
    return jnp.stack([am, bm, ck, sk, two(aq), two(bq), two(ak), two(bk)], axis=0)


def _layer_weights(l, w_in, mla_w_qb, mla_w_kvb, w_o_mla, w_o_gqa, w_out):
    d = w_in.shape[1]
    wl = w_in[l]
    offs = np.cumsum([0, MLA_Q_LORA, MLA_KV_LORA, MLA_ROPE, 512, 128, 128, d, d])
    qc, kvc, kpe, qg, kg, vg, gm, gg = [wl[:, offs[i]:offs[i + 1]] for i in range(8)]
    pm = _swap_perm(MLA_ROPE)
    pg = _swap_perm(GQA_HEAD_DIM)
    pg8 = np.concatenate([pg + 64 * h for h in range(GQA_HEADS)])
    pg2 = np.concatenate([pg + 64 * h for h in range(GQA_KV_HEADS)])

    def pad_kpe(w):
        return jnp.pad(w, ((0, 0), (64, 32)))

    def dup(w):
        return jnp.concatenate([w[:, :64], w[:, :64], w[:, 64:], w[:, 64:]], axis=1)

    w_ext = jnp.concatenate(
        [qc, kvc, pad_kpe(kpe), pad_kpe(kpe[:, pm]), qg, qg[:, pg8],
         dup(kg), dup(kg[:, pg2]), gm, gg], axis=1).astype(BF16)

    wqb = mla_w_qb[l].reshape(MLA_Q_LORA, MLA_HEADS, MLA_NOPE + MLA_ROPE)
    rope_sw = wqb[:, :, MLA_NOPE:][:, :, pm]
    wqb_pad = jnp.pad(wqb, ((0, 0), (0, 0), (0, 32))).reshape(MLA_Q_LORA, 1024)
    wqbs_pad = jnp.pad(rope_sw, ((0, 0), (0, 0), (64, 32))).reshape(MLA_Q_LORA, 1024)

    wkvb = mla_w_kvb[l].reshape(MLA_KV_LORA, MLA_HEADS, MLA_NOPE + MLA_V)
    wkk = jnp.pad(wkvb[:, :, :MLA_NOPE], ((0, 0), (0, 0), (0, 64))).reshape(MLA_KV_LORA, 1024)
    wkv = wkvb[:, :, MLA_NOPE:].reshape(MLA_KV_LORA, MLA_HEADS * MLA_V)

    def pad_rows(w, heads):
        w = jnp.pad(w.reshape(heads, 64, d), ((0, 0), (0, 64), (0, 0)))
        return w.reshape(heads * LANE, d)

    wom = pad_rows(w_o_mla[l], MLA_HEADS)
    wog = pad_rows(w_o_gqa[l], GQA_HEADS)

    eye = np.kron(np.eye(8, dtype=np.float32), np.full((64, 64), 1.0 / 64, np.float32))
    return {
        "w_in": w_ext,
        "wqb": wqb_pad.astype(BF16),
        "wqbs": wqbs_pad.astype(BF16),
        "wkk": wkk.astype(BF16),
        "wkvt": wkv.T.astype(BF16),
        "wvgt": vg.T.astype(BF16),
        "bd": jnp.asarray(eye, BF16),
        "wom": wom.astype(BF16),
        "wog": wog.astype(BF16),
        "wout": w_out[l].astype(BF16),
    }


def kernel(x, c, ctx, c_ctx, w_ada, b_ada, norm1_g, norm2_g, w_in, mla_q_norm_g, mla_w_qb,
           mla_kv_norm_g, mla_w_kvb, gqa_q_norm_g, gqa_k_norm_g, w_o_mla, w_o_gqa, w_out,
           dense_w1, dense_w3, dense_w2, moe_router, moe_w1, moe_w3, moe_w2, final_norm_g):
    b, n_lat, d = x.shape
    n_ctx = ctx.shape[1]
    depth = w_ada.shape[0]
    t = n_ctx + n_lat
    tm, tq = TOKEN_TILE, QUERY_TILE
    assert n_ctx % tm == 0 and n_lat % tm == 0 and n_lat % GRID_W == 0
    assert n_lat % (2 * tq) == 0 and n_lat % n_ctx == 0 and n_ctx % KEY_BLOCK == 0

    rows = -(-(b + 1) // 8) * 8
    cc = jnp.zeros((rows, d), F32).at[:b].set(c).at[b].set(c_ctx)
    mod_all = _ada_call(cc, w_ada, b_ada).reshape(depth, rows, 6, d)

    cos_m, sin_m = _rope_parts(n_ctx, n_lat, MLA_ROPE)
    cos_g, sin_g = _rope_parts(n_ctx, n_lat, GQA_HEAD_DIM)

    xs = jnp.concatenate([x, ctx], axis=1)
    for l in range(depth):
        lw = _layer_weights(l, w_in, mla_w_qb, mla_w_kvb, w_o_mla, w_o_gqa, w_out)
        lw["gq"] = mla_q_norm_g[l][None, :]
        lw["gkv"] = mla_kv_norm_g[l][None, :]
        lw["g2"] = norm2_g[l][None, :]
        tabs = _layer_tables(cos_m, sin_m, cos_g, sin_g, gqa_q_norm_g[l], gqa_k_norm_g[l])
        mod = mod_all[l]
        moe = l % 2 == 1

        qm, km, vm, qg, kg, vg, gates = _inproj_call(
            xs, mod, norm1_g[l][None, :], lw, tabs, n_ctx=n_ctx, tm=tm)
        om = _attn_call(qm, km, vm, gqa=False, n_ctx=n_ctx, tq=tq)
        og = _attn_call(qg, kg, vg, gqa=True, n_ctx=n_ctx, tq=tq)

        if moe:
            r = jnp.pad(moe_router[l // 2], ((0, 0), (0, LANE - N_EXPERTS)))
            lw["r_hi"] = r.astype(BF16)
            lw["r_lo"] = (r - lw["r_hi"].astype(F32)).astype(BF16)
            xs, h2, comb = _post_call(om, og, gates, xs, mod, lw, n_ctx=n_ctx, tm=tm, moe=True)
            lw["w1"] = moe_w1[l // 2].astype(BF16)
            lw["w3"] = moe_w3[l // 2].astype(BF16)
            lw["w2"] = moe_w2[l // 2].astype(BF16)
            f = _moe_ffn_call(h2.reshape(b * t, d), comb.reshape(b * t, LANE), lw)
            f = f.reshape(b, t, d)
            if l == depth - 1:
                return _final_norm_call(xs, final_norm_g[None, :], n_ctx=n_ctx, tm=tm,
                                        f=f, mod=mod)
            xs = _residual_call(xs, f, mod, n_ctx=n_ctx, tm=tm)
        else:
            xs, h2 = _post_call(om, og, gates, xs, mod, lw, n_ctx=n_ctx, tm=tm, moe=False)
            lw["w1"] = dense_w1[l // 2].astype(BF16)
            lw["w3"] = dense_w3[l // 2].astype(BF16)
            lw["w2"] = dense_w2[l // 2].astype(BF16)
            xs = _dense_ffn_call(h2, xs, mod, lw, n_ctx=n_ctx, tm=tm)

    return _final_norm_call(xs, final_norm_g[None, :], n_ctx=n_ctx, tm=tm)
```

```python
import functools
import math

import numpy as np
import jax
import jax.numpy as jnp
from jax import lax
from jax.experimental import pallas as pl
from jax.experimental.pallas import tpu as pltpu

F32 = jnp.float32
BF16 = jnp.bfloat16

GRID_W = 64
ROPE_THETA = 10000.0
NORM_EPS = 1e-6
MLA_HEADS = 8
MLA_Q_LORA = 384
MLA_KV_LORA = 256
MLA_NOPE = 64
MLA_ROPE = 32
MLA_V = 64
MLA_SCALE = (MLA_NOPE + MLA_ROPE) ** -0.5
GQA_HEADS = 8
GQA_KV_HEADS = 2
GQA_GROUP = GQA_HEADS // GQA_KV_HEADS
GQA_HEAD_DIM = 64
GQA_SCALE = GQA_HEAD_DIM ** -0.5
N_EXPERTS = 8
LOG2E = math.log2(math.e)

LANE = 128
VMEM_LIMIT = 56 * 1024 * 1024
TOKEN_TILE = 256
QUERY_TILE = 256

C_QC = 0
C_KVC = C_QC + MLA_Q_LORA
C_KPE = C_KVC + MLA_KV_LORA
C_KPES = C_KPE + LANE
C_QG = C_KPES + LANE
C_QGS = C_QG + 512
C_KG = C_QGS + 512
C_KGS = C_KG + 256
C_GATE = C_KGS + 256
C_END = C_GATE + 2048

T_AM, T_BM, T_CK, T_SK, T_AQ, T_BQ, T_AK, T_BK = range(8)


def _cparams(sem):
    return pltpu.CompilerParams(dimension_semantics=sem, vmem_limit_bytes=VMEM_LIMIT)


def _const_spec(shape):
    nd = len(shape)
    return pl.BlockSpec(shape, lambda *_: (0,) * nd)


def _dot(a, b):
    return jnp.dot(a, b, preferred_element_type=F32)


def _rms(x):
    return x * lax.rsqrt(jnp.mean(x * x, axis=-1, keepdims=True) + NORM_EPS)


def _ada_kernel(cc_ref, w_ref, b_ref, o_ref):
    a = cc_ref[...]
    a = a * jax.nn.sigmoid(a)
    a_hi = a.astype(BF16)
    a_lo = (a - a_hi.astype(F32)).astype(BF16)
    w = w_ref[0]
    w_hi = w.astype(BF16)
    w_lo = (w - w_hi.astype(F32)).astype(BF16)
    acc = _dot(a_hi, w_hi) + _dot(a_lo, w_hi) + _dot(a_hi, w_lo)
    o_ref[0] = acc + b_ref[0]


def _ada_call(cc, w_ada, b_ada):
    depth, d, n = w_ada.shape
    rows = cc.shape[0]
    tn = 1536
    return pl.pallas_call(
        _ada_kernel,
        grid=(depth, n // tn),
        in_specs=[
            pl.BlockSpec((rows, d), lambda l, j: (0, 0)),
            pl.BlockSpec((1, d, tn), lambda l, j: (l, 0, j)),
            pl.BlockSpec((1, 1, tn), lambda l, j: (l, 0, j)),
        ],
        out_specs=pl.BlockSpec((1, rows, tn), lambda l, j: (l, 0, j)),
        out_shape=jax.ShapeDtypeStruct((depth, rows, n), F32),
        compiler_params=_cparams(("arbitrary", "arbitrary")),
        name="ada",
    )(cc, w_ada, b_ada.reshape(depth, 1, n))


def _group_mean_sq(x, bd):
    x2 = x * x
    hi = x2.astype(BF16)
    lo = (x2 - hi.astype(F32)).astype(BF16)
    return _dot(hi, bd) + _dot(lo, bd)


def _tile_lanes(t, n):
    return jnp.concatenate([t] * n, axis=1)


def _dot_nt(a, b):
    return lax.dot_general(a, b, (((1,), (1,)), ((), ())), preferred_element_type=F32)


def _inproj_kernel(x_ref, mod_ref, g1_ref, win_ref, gq_ref, wqb_ref, wqbs_ref, gkv_ref,
                   wkk_ref, wkvt_ref, wvgt_ref, bd_ref, tab_ref,
                   qm_ref, km_ref, vm_ref, qg_ref, kg_ref, vg_ref, gate_ref):
    x = x_ref[0]
    mod = mod_ref[0]
    h = (_rms(x) * g1_ref[...]) * (1.0 + mod[1:2]) + mod[0:1]
    hb = h.astype(BF16)

    y = _dot(hb, win_ref[...])

    def proj(lo, hi):
        return y[:, lo:hi]

    qcn = (_rms(proj(C_QC, C_KVC)) * gq_ref[...]).astype(BF16)
    q = _dot(qcn, wqb_ref[...])
    qs = _dot(qcn, wqbs_ref[...])
    am = _tile_lanes(tab_ref[T_AM], MLA_HEADS)
    bm = _tile_lanes(tab_ref[T_BM], MLA_HEADS)
    qm_ref[0] = (q * am + qs * bm).astype(BF16)

    kvn = (_rms(proj(C_KVC, C_KPE)) * gkv_ref[...]).astype(BF16)
    kpe = proj(C_KPE, C_KPES) * tab_ref[T_CK] + proj(C_KPES, C_QG) * tab_ref[T_SK]
    kn = _dot(kvn, wkk_ref[...])
    km_ref[0] = (kn + _tile_lanes(kpe, MLA_HEADS)).astype(BF16)
    tm = x.shape[0]
    vm_ref[0] = _dot_nt(wkvt_ref[...], kvn).astype(BF16).reshape(MLA_HEADS, MLA_V, tm)

    bd = bd_ref[...]
    qg = proj(C_QG, C_QGS)
    qgs = proj(C_QGS, C_KG)
    rq = lax.rsqrt(_group_mean_sq(qg, bd) + NORM_EPS)
    aq = _tile_lanes(tab_ref[T_AQ], 4)
    bq = _tile_lanes(tab_ref[T_BQ], 4)
    qg_ref[0] = (rq * (qg * aq + qgs * bq)).astype(BF16)

    kg = proj(C_KG, C_KGS)
    kgs = proj(C_KGS, C_GATE)
    rk = lax.rsqrt(_group_mean_sq(kg, bd[:256, :256]) + NORM_EPS)
    ak = _tile_lanes(tab_ref[T_AK], 2)
    bk = _tile_lanes(tab_ref[T_BK], 2)
    kg_ref[0] = (rk * (kg * ak + kgs * bk)).astype(BF16)
    vg_ref[0] = _dot_nt(wvgt_ref[...], hb).astype(BF16).reshape(
        GQA_KV_HEADS, GQA_HEAD_DIM, tm)

    gate_ref[0] = jax.nn.sigmoid(proj(C_GATE, C_END)).astype(BF16)


def _inproj_call(xs, mod, g1, lw, tabs, *, n_ctx, tm):
    b, t, d = xs.shape
    nlt = (t - n_ctx) // tm
    ctx_row = b

    def tok(i, bb):
        return (bb, i, 0)

    def modmap(i, bb):
        return (jnp.where(i >= nlt, ctx_row, bb), 0, 0)

    def tok_spec(w):
        return pl.BlockSpec((1, tm, w), tok), jax.ShapeDtypeStruct((b, t, w), BF16)

    def vt_spec(heads):
        return (pl.BlockSpec((1, heads, 64, tm), lambda i, bb: (bb, 0, 0, i)),
                jax.ShapeDtypeStruct((b, heads, 64, t), BF16))

    outs = [tok_spec(1024), tok_spec(1024), vt_spec(MLA_HEADS),
            tok_spec(512), tok_spec(256), vt_spec(GQA_KV_HEADS), tok_spec(2048)]
    return pl.pallas_call(
        _inproj_kernel,
        grid=(t // tm, b),
        in_specs=[
            pl.BlockSpec((1, tm, d), tok),
            pl.BlockSpec((1, 6, d), modmap),
            _const_spec((1, d)),
            _const_spec((d, C_END)),
            _const_spec((1, MLA_Q_LORA)),
            _const_spec((MLA_Q_LORA, 1024)),
            _const_spec((MLA_Q_LORA, 1024)),
            _const_spec((1, MLA_KV_LORA)),
            _const_spec((MLA_KV_LORA, 1024)),
            _const_spec((MLA_HEADS * MLA_V, MLA_KV_LORA)),
            _const_spec((GQA_KV_HEADS * GQA_HEAD_DIM, d)),
            _const_spec((512, 512)),
            pl.BlockSpec((8, tm, LANE), lambda i, bb: (0, i, 0)),
        ],
        out_specs=[o[0] for o in outs],
        out_shape=[o[1] for o in outs],
        compiler_params=_cparams(("parallel", "arbitrary")),
        name="inproj",
    )(xs, mod, g1, lw["w_in"], lw["gq"], lw["wqb"], lw["wqbs"], lw["gkv"],
      lw["wkk"], lw["wkvt"], lw["wvgt"], lw["bd"], tabs)


KEY_BLOCK = 256


def _key_block(nkeys):
    assert nkeys % KEY_BLOCK == 0, nkeys
    return KEY_BLOCK


def _mask_gqa_q(q):
    lane = lax.broadcasted_iota(jnp.int32, q.shape, 1)
    keep = (lane // 64) == (pl.program_id(1) % 2)
    return jnp.where(keep, q, jnp.zeros_like(q))


def _attn_scores(q, k_ref, s_ref, m_ref, nkeys, probs=None, zero_ref=None):
    kb = _key_block(nkeys)
    mrun = lrun = None
    if probs is not None:
        s2_ref, p2_ref, m2_ref, l2_ref = probs
        m2 = m2_ref[0:1, :]
        zero = zero_ref[0:1, :]
    for lo in range(0, nkeys, kb):
        k = k_ref[0, lo:lo + kb, :]
        if probs is not None:
            p = jnp.exp2(s2_ref[lo:lo + kb, :] - m2)
            p2_ref[lo:lo + kb, :] = p.astype(BF16)
            f = jnp.sum(p, axis=0, keepdims=True)
            lrun = f if lrun is None else lrun + f
            tie = pltpu.bitcast(f, jnp.uint32) & zero
            tie = tie[:, :LANE] | tie[:, LANE:]
            k = pltpu.bitcast(pltpu.bitcast(k, jnp.uint32) | tie, BF16)
        s = lax.dot_general(k, q, (((1,), (1,)), ((), ())), preferred_element_type=F32)
        s_ref[lo:lo + kb, :] = s
        f = jnp.max(s, axis=0, keepdims=True)
        mrun = f if mrun is None else jnp.maximum(mrun, f)
    m_ref[...] = jnp.broadcast_to(mrun, m_ref.shape)
    if probs is not None:
        l2_ref[...] = jnp.broadcast_to(lrun, l2_ref.shape)


def _attn_probs(s_ref, p_ref, m_ref, l_ref, nkeys):
    kb = _key_block(nkeys)
    m = m_ref[0:1, :]
    lrun = None
    for lo in range(0, nkeys, kb):
        p = jnp.exp2(s_ref[lo:lo + kb, :] - m)
        f = jnp.sum(p, axis=0, keepdims=True)
        lrun = f if lrun is None else lrun + f
        p_ref[lo:lo + kb, :] = p.astype(BF16)
    l_ref[...] = jnp.broadcast_to(lrun, l_ref.shape)


def _attn_values(p_ref, vt_ref, l_ref, nkeys):
    kb = _key_block(nkeys)
    acc = None
    for lo in range(0, nkeys, kb):
        part = _dot(vt_ref[0, 0, :, lo:lo + kb], p_ref[lo:lo + kb, :])
        acc = part if acc is None else acc + part
    o_t = acc / l_ref[0:1, :]
    o_t = jnp.concatenate([o_t, jnp.zeros_like(o_t)], axis=0)
    return o_t.T.astype(BF16)


def _attn_ctx_kernel(q_ref, k_ref, vt_ref, o_ref, s_scr, p_scr, m_scr, l_scr, *, gqa):
    q = q_ref[0]
    if gqa:
        q = _mask_gqa_q(q)
    nkeys = k_ref.shape[1]
    _attn_scores(q, k_ref, s_scr, m_scr, nkeys)
    _attn_probs(s_scr, p_scr, m_scr, l_scr, nkeys)
    o_ref[0] = _attn_values(p_scr, vt_ref, l_scr, nkeys)


def _attn_lat_kernel(q_ref, k_ref, vt_ref, zero_ref, oin_ref, o_ref,
                     s0, s1, p0, p1, m0, m1, l0, l1, *, gqa, tq, nkeys):
    del oin_ref
    j = pl.program_id(2)
    last = pl.num_programs(2) - 1
    s_scr, p_scr, m_scr, l_scr = (s0, s1), (p0, p1), (m0, m1), (l0, l1)

    def scores(slot, tied_probs=False):
        q = q_ref[0, slot * tq:(slot + 1) * tq, :]
        if gqa:
            q = _mask_gqa_q(q)
        other = None
        if tied_probs:
            o = 1 - slot
            other = (s_scr[o], p_scr[o], m_scr[o], l_scr[o])
        _attn_scores(q, k_ref, s_scr[slot], m_scr[slot], nkeys, other, zero_ref)

    def probs(slot):
        _attn_probs(s_scr[slot], p_scr[slot], m_scr[slot], l_scr[slot], nkeys)

    def values(slot):
        o_ref[0, slot * tq:(slot + 1) * tq, :] = _attn_values(
            p_scr[slot], vt_ref, l_scr[slot], nkeys)

    @pl.when(j == 0)
    def _():
        scores(0)
        scores(1)
        probs(0)

    @pl.when((j > 0) & (j < last))
    def _():
        scores(0, tied_probs=True)
        values(0)
        scores(1)
        probs(0)
        values(1)

    @pl.when(j == last)
    def _():
        probs(1)
        values(0)
        values(1)


def _attn_call(q, k, vt, *, gqa, n_ctx, tq):
    b, t, _ = q.shape
    n_lat = t - n_ctx
    heads = GQA_HEADS if gqa else MLA_HEADS
    if gqa:
        qcol = lambda h: h // 2
        kcol = lambda h: h // GQA_GROUP
        vhead = kcol
    else:
        qcol = lambda h: h
        kcol = lambda h: h
        vhead = lambda h: h
    out_shape = jax.ShapeDtypeStruct((b, t, heads * LANE), BF16)
    sem = ("parallel", "parallel", "arbitrary")

    cblk = n_lat // n_ctx
    o_ctx = pl.pallas_call(
        functools.partial(_attn_ctx_kernel, gqa=gqa),
        grid=(b, heads),
        in_specs=[
            pl.BlockSpec((1, n_ctx, LANE), lambda bb, h: (bb, cblk, qcol(h))),
            pl.BlockSpec((1, n_ctx, LANE), lambda bb, h: (bb, cblk, kcol(h))),
            pl.BlockSpec((1, 1, 64, n_ctx), lambda bb, h: (bb, vhead(h), 0, cblk)),
        ],
        out_specs=pl.BlockSpec((1, n_ctx, LANE), lambda bb, h: (bb, cblk, h)),
        out_shape=out_shape,
        scratch_shapes=[pltpu.VMEM((n_ctx, n_ctx), F32), pltpu.VMEM((n_ctx, n_ctx), BF16),
                        pltpu.VMEM((8, n_ctx), F32), pltpu.VMEM((8, n_ctx), F32)],
        compiler_params=_cparams(sem[:2]),
        name="attn_ctx_gqa" if gqa else "attn_ctx_mla",
    )(q, k, vt)

    npairs = n_lat // (2 * tq)
    kern = functools.partial(_attn_lat_kernel, gqa=gqa, tq=tq, nkeys=t)
    return pl.pallas_call(
        kern,
        grid=(b, heads, npairs + 1),
        in_specs=[
            pl.BlockSpec((1, 2 * tq, LANE),
                         lambda bb, h, j: (bb, jnp.minimum(j, npairs - 1), qcol(h))),
            pl.BlockSpec((1, t, LANE), lambda bb, h, j: (bb, 0, kcol(h))),
            pl.BlockSpec((1, 1, 64, t), lambda bb, h, j: (bb, vhead(h), 0, 0)),
            pl.BlockSpec((8, tq), lambda bb, h, j: (0, 0)),
            pl.BlockSpec(memory_space=pl.ANY),
        ],
        out_specs=pl.BlockSpec((1, 2 * tq, LANE),
                               lambda bb, h, j: (bb, jnp.maximum(j - 1, 0), h)),
        out_shape=out_shape,
        scratch_shapes=[pltpu.VMEM((t, tq), F32)] * 2 + [pltpu.VMEM((t, tq), BF16)] * 2
        + [pltpu.VMEM((8, tq), F32)] * 4,
        input_output_aliases={4: 0},
        compiler_params=_cparams(sem),
        name="attn_gqa" if gqa else "attn_mla",
    )(q, k, vt, jnp.zeros((8, tq), jnp.uint32), o_ctx)


def _top2_combine(logits):
    lane = lax.broadcasted_iota(jnp.int32, logits.shape, 1)
    neg = jnp.float32(-jnp.inf)
    lg = jnp.where(lane < N_EXPERTS, logits, neg)
    m1 = jnp.max(lg, axis=-1, keepdims=True)
    i1 = jnp.min(jnp.where(lg == m1, lane, LANE), axis=-1, keepdims=True)
    sel1 = lane == i1
    lg2 = jnp.where(sel1, neg, lg)
    m2 = jnp.max(lg2, axis=-1, keepdims=True)
    i2 = jnp.min(jnp.where(lg2 == m2, lane, LANE), axis=-1, keepdims=True)
    sel2 = lane == i2
    e2 = jnp.exp(m2 - m1)
    w1 = 1.0 / (1.0 + e2)
    w2 = e2 / (1.0 + e2)
    return jnp.where(sel1, w1, 0.0) + jnp.where(sel2, w2, 0.0)


def _post_kernel(*refs, moe):
    if moe:
        (om_ref, og_ref, gate_ref, x_ref, mod_ref, wom_ref, wog_ref, wout_ref, g2_ref,
         rhi_ref, rlo_ref, xo_ref, h2_ref, comb_ref) = refs
    else:
        (om_ref, og_ref, gate_ref, x_ref, mod_ref, wom_ref, wog_ref, wout_ref, g2_ref,
         xo_ref, h2_ref) = refs
    a = _dot(om_ref[0], wom_ref[...])
    bb = _dot(og_ref[0], wog_ref[...])
    gate = gate_ref[0]
    merged = gate[:, :1024].astype(F32) * a + gate[:, 1024:].astype(F32) * bb
    mix = _dot(merged.astype(BF16), wout_ref[...])
    mod = mod_ref[0]
    x1 = x_ref[0] + mod[2:3] * mix
    xo_ref[0] = x1
    h2 = (_rms(x1) * g2_ref[...]) * (1.0 + mod[4:5]) + mod[3:4]
    h2_ref[0] = h2.astype(BF16)
    if moe:
        h_hi = h2.astype(BF16)
        h_lo = (h2 - h_hi.astype(F32)).astype(BF16)
        logits = (_dot(h_hi, rhi_ref[...]) + _dot(h_lo, rhi_ref[...])
                  + _dot(h_hi, rlo_ref[...]))
        comb_ref[0] = _top2_combine(logits)


def _post_call(om, og, gates, xs, mod, lw, *, n_ctx, tm, moe):
    b, t, d = xs.shape
    nlt = (t - n_ctx) // tm
    ctx_row = b

    def tok(i, bb):
        return (bb, i, 0)

    def modmap(i, bb):
        return (jnp.where(i >= nlt, ctx_row, bb), 0, 0)

    in_specs = [
        pl.BlockSpec((1, tm, MLA_HEADS * LANE), tok),
        pl.BlockSpec((1, tm, GQA_HEADS * LANE), tok),
        pl.BlockSpec((1, tm, 2048), tok),
        pl.BlockSpec((1, tm, d), tok),
        pl.BlockSpec((1, 6, d), modmap),
        _const_spec((MLA_HEADS * LANE, d)),
        _const_spec((GQA_HEADS * LANE, d)),
        _const_spec((d, d)),
        _const_spec((1, d)),
    ]
    args = [om, og, gates, xs, mod, lw["wom"], lw["wog"], lw["wout"], lw["g2"]]
    out_specs = [pl.BlockSpec((1, tm, d), tok), pl.BlockSpec((1, tm, d), tok)]
    out_shape = [jax.ShapeDtypeStruct((b, t, d), F32), jax.ShapeDtypeStruct((b, t, d), BF16)]
    if moe:
        in_specs += [_const_spec((d, LANE)), _const_spec((d, LANE))]
        args += [lw["r_hi"], lw["r_lo"]]
        out_specs.append(pl.BlockSpec((1, tm, LANE), tok))
        out_shape.append(jax.ShapeDtypeStruct((b, t, LANE), F32))
    return pl.pallas_call(
        functools.partial(_post_kernel, moe=moe),
        grid=(t // tm, b),
        in_specs=in_specs,
        out_specs=out_specs,
        out_shape=out_shape,
        input_output_aliases={3: 0},
        compiler_params=_cparams(("parallel", "arbitrary")),
        name="post_moe" if moe else "post_dense",
    )(*args)


def _swiglu_chunk(hb, w1, w3, w2):
    a = _dot(hb, w1)
    g = (a * jax.nn.sigmoid(a)) * _dot(hb, w3)
    return _dot(g.astype(BF16), w2)


def _dense_ffn_kernel(h_ref, x_ref, mod_ref, w1_ref, w3_ref, w2_ref, o_ref):
    f = _swiglu_chunk(h_ref[0], w1_ref[...], w3_ref[...], w2_ref[...])
    o_ref[0] = x_ref[0] + mod_ref[0][5:6] * f


def _dense_ffn_call(h2, xs, mod, lw, *, n_ctx, tm):
    b, t, d = xs.shape
    f = lw["w1"].shape[1]
    nlt = (t - n_ctx) // tm
    ctx_row = b

    def tok(i, bb):
        return (bb, i, 0)

    def modmap(i, bb):
        return (jnp.where(i >= nlt, ctx_row, bb), 0, 0)

    return pl.pallas_call(
        _dense_ffn_kernel,
        grid=(t // tm, b),
        in_specs=[
            pl.BlockSpec((1, tm, d), tok),
            pl.BlockSpec((1, tm, d), tok),
            pl.BlockSpec((1, 6, d), modmap),
            pl.BlockSpec((d, f), lambda i, bb: (0, 0), pipeline_mode=pl.Buffered(1)),
            pl.BlockSpec((d, f), lambda i, bb: (0, 0), pipeline_mode=pl.Buffered(1)),
            pl.BlockSpec((f, d), lambda i, bb: (0, 0), pipeline_mode=pl.Buffered(1)),
        ],
        out_specs=pl.BlockSpec((1, tm, d), tok),
        out_shape=jax.ShapeDtypeStruct((b, t, d), F32),
        input_output_aliases={1: 0},
        compiler_params=_cparams(("parallel", "arbitrary")),
        name="ffn_dense",
    )(h2, xs, mod, lw["w1"], lw["w3"], lw["w2"])


MOE_F_SPLIT = 2


def _moe_tile(n):
    for tb in (1024, 768, 512, 256):
        if n % tb == 0:
            return tb
    raise ValueError(n)


def _moe_rows(tb):
    half = tb // N_EXPERTS
    return -(-(half + half // 8) // 16) * 16


def _moe_ffn_kernel(h_ref, comb_ref, tri_ref, w1_ref, w3_ref, w2_ref, o_ref,
                    sel_scr, pos_scr, xg_scr, y_scr, *, rows):
    e = pl.program_id(1)
    fh = pl.program_id(2)
    tb = h_ref.shape[0]

    @pl.when((e == 0) & (fh == 0))
    def _():
        sel_t = (comb_ref[...].T[:16] > 0.0).astype(BF16)
        sel_scr[...] = sel_t.astype(F32)
        pos_scr[...] = lax.dot_general(sel_t, tri_ref[...], (((1,), (1,)), ((), ())),
                                       preferred_element_type=F32)
        o_ref[...] = jnp.zeros_like(o_ref)

    sel_e = sel_scr[pl.ds(e, 1), :]
    rank_e = jnp.where(sel_e > 0.0, pos_scr[pl.ds(e, 1), :], -1.0)
    count = jnp.sum(sel_e).astype(jnp.int32)
    nsub = (count + rows - 1) // rows

    def one_hot(s):
        j = lax.broadcasted_iota(jnp.int32, (rows, tb), 0) + s * rows
        return jnp.where(rank_e == j.astype(F32), 1.0, 0.0).astype(BF16)

    def block(s):
        return pl.ds(pl.multiple_of(s * rows, 16), rows)

    @pl.when(fh == 0)
    def _():
        def gather(s, carry):
            xg_scr[block(s), :] = _dot(one_hot(s), h_ref[...]).astype(BF16)
            return carry
        lax.fori_loop(0, nsub, gather, 0)

    def expert(s, carry):
        part = _swiglu_chunk(xg_scr[block(s), :], w1_ref[0], w3_ref[0], w2_ref[0])

        @pl.when(fh == 0)
        def _():
            y_scr[block(s), :] = part

        @pl.when(fh != 0)
        def _():
            y_scr[block(s), :] = y_scr[block(s), :] + part
        return carry
    lax.fori_loop(0, nsub, expert, 0)

    @pl.when(fh == pl.num_programs(2) - 1)
    def _():
        comb = comb_ref[...]
        lane = lax.broadcasted_iota(jnp.int32, comb.shape, 1)
        w_col = jnp.sum(jnp.where(lane == e, comb, 0.0), axis=-1, keepdims=True)

        def scatter(s, carry):
            z = lax.dot_general(one_hot(s), y_scr[block(s), :].astype(BF16),
                                (((0,), (0,)), ((), ())), preferred_element_type=F32)
            o_ref[...] += w_col * z
            return carry
        lax.fori_loop(0, nsub, scatter, 0)


def _moe_ffn_call(h2, comb, lw):
    n, d = h2.shape
    ne, _, f = lw["w1"].shape
    fc = f // MOE_F_SPLIT
    tb = _moe_tile(n)
    rows = _moe_rows(tb)
    nsub_max = -(-tb // rows)
    tri = (lax.broadcasted_iota(jnp.int32, (tb, tb), 1)
           < lax.broadcasted_iota(jnp.int32, (tb, tb), 0)).astype(BF16)
    return pl.pallas_call(
        functools.partial(_moe_ffn_kernel, rows=rows),
        grid=(n // tb, ne, MOE_F_SPLIT),
        in_specs=[
            pl.BlockSpec((tb, d), lambda i, e, fh: (i, 0)),
            pl.BlockSpec((tb, LANE), lambda i, e, fh: (i, 0)),
            pl.BlockSpec((tb, tb), lambda i, e, fh: (0, 0), pipeline_mode=pl.Buffered(1)),
            pl.BlockSpec((1, d, fc), lambda i, e, fh: (e, 0, fh)),
            pl.BlockSpec((1, d, fc), lambda i, e, fh: (e, 0, fh)),
            pl.BlockSpec((1, fc, d), lambda i, e, fh: (e, fh, 0)),
        ],
        out_specs=pl.BlockSpec((tb, d), lambda i, e, fh: (i, 0)),
        out_shape=jax.ShapeDtypeStruct((n, d), F32),
        scratch_shapes=[pltpu.VMEM((16, tb), F32), pltpu.VMEM((16, tb), F32),
                        pltpu.VMEM((nsub_max * rows, d), BF16),
                        pltpu.VMEM((nsub_max * rows, d), F32)],
        compiler_params=_cparams(("parallel", "arbitrary", "arbitrary")),
        name="ffn_moe",
    )(h2, comb, tri, lw["w1"], lw["w3"], lw["w2"])


def _residual_kernel(x_ref, f_ref, mod_ref, o_ref):
    o_ref[0] = x_ref[0] + mod_ref[0][5:6] * f_ref[0]


def _residual_call(xs, f, mod, *, n_ctx, tm):
    b, t, d = xs.shape
    nlt = (t - n_ctx) // tm
    ctx_row = b

    def tok(i, bb):
        return (bb, i, 0)

    def modmap(i, bb):
        return (jnp.where(i >= nlt, ctx_row, bb), 0, 0)

    return pl.pallas_call(
        _residual_kernel,
        grid=(t // tm, b),
        in_specs=[pl.BlockSpec((1, tm, d), tok), pl.BlockSpec((1, tm, d), tok),
                  pl.BlockSpec((1, 6, d), modmap)],
        out_specs=pl.BlockSpec((1, tm, d), tok),
        out_shape=jax.ShapeDtypeStruct((b, t, d), F32),
        input_output_aliases={0: 0},
        compiler_params=_cparams(("parallel", "arbitrary")),
        name="moe_residual",
    )(xs, f, mod)


def _final_norm_kernel(x_ref, g_ref, o_ref):
    o_ref[0] = _rms(x_ref[0]) * g_ref[...]


def _final_norm_res_kernel(x_ref, g_ref, f_ref, mod_ref, o_ref):
    o_ref[0] = _rms(x_ref[0] + mod_ref[0][5:6] * f_ref[0]) * g_ref[...]


def _final_norm_call(xs, g, *, n_ctx, tm, f=None, mod=None):
    b, t, d = xs.shape
    n_lat = t - n_ctx
    tok = pl.BlockSpec((1, tm, d), lambda i, bb: (bb, i, 0))
    in_specs = [tok, _const_spec((1, d))]
    args = [xs, g]
    kern = _final_norm_kernel
    if f is not None:
        in_specs += [tok, pl.BlockSpec((1, 6, d), lambda i, bb: (bb, 0, 0))]
        args += [f, mod]
        kern = _final_norm_res_kernel
    return pl.pallas_call(
        kern,
        grid=(n_lat // tm, b),
        in_specs=in_specs,
        out_specs=tok,
        out_shape=jax.ShapeDtypeStruct((b, n_lat, d), F32),
        compiler_params=_cparams(("parallel", "arbitrary")),
        name="final_norm",
    )(*args)


def _rope_parts(n_ctx, n_lat, dim):
    half = dim // 2
    quarter = half // 2
    inv = ROPE_THETA ** (-(jnp.arange(quarter, dtype=F32) * 2.0) / half)
    tpos = jnp.arange(n_lat)
    row = (tpos // GRID_W).astype(F32)
    col = (tpos % GRID_W).astype(F32)
    zeros = jnp.zeros((n_ctx, quarter), F32)
    ang_r = jnp.concatenate([row[:, None] * inv, zeros], axis=0)
    ang_c = jnp.concatenate([col[:, None] * inv, zeros], axis=0)
    cr, sr, cc, sc = jnp.cos(ang_r), jnp.sin(ang_r), jnp.cos(ang_c), jnp.sin(ang_c)
    cos = jnp.concatenate([cr, cr, cc, cc], axis=1)
    sin = jnp.concatenate([-sr, sr, -sc, sc], axis=1)
    return cos, sin


def _swap_perm(dim):
    q = dim // 4
    idx = np.arange(dim)
    return np.concatenate([idx[q:2 * q], idx[:q], idx[3 * q:], idx[2 * q:3 * q]])


def _layer_tables(cos_m, sin_m, cos_g, sin_g, gq, gk):
    t = cos_m.shape[0]
    z64 = jnp.zeros((t, 64), F32)
    z32 = jnp.zeros((t, 32), F32)
    sm = MLA_SCALE * LOG2E
    sg = GQA_SCALE * LOG2E
    pg = _swap_perm(GQA_HEAD_DIM)
    am = jnp.concatenate([jnp.full((t, 64), sm, F32), sm * cos_m, z32], axis=1)
    bm = jnp.concatenate([z64, sm * sin_m, z32], axis=1)
    ck = jnp.concatenate([z64, cos_m, z32], axis=1)
    sk = jnp.concatenate([z64, sin_m, z32], axis=1)
    aq = sg * cos_g * gq[None, :]
    bq = sg * sin_g * gq[pg][None, :]
    ak = cos_g * gk[None, :]
    bk = sin_g * gk[pg][None, :]
    two = lambda a: jnp.concatenate([a, a], axis=1)
    return jnp.stack([am, bm, ck, sk, two(aq), two(bq), two(ak), two(bk)], axis=0)


def _layer_weights(l, w_in, mla_w_qb, mla_w_kvb, w_o_mla, w_o_gqa, w_out):
    d = w_in.shape[1]
    wl = w_in[l]
    offs = np.cumsum([0, MLA_Q_LORA, MLA_KV_LORA, MLA_ROPE, 512, 128, 128, d, d])
    qc, kvc, kpe, qg, kg, vg, gm, gg = [wl[:, offs[i]:offs[i + 1]] for i in range(8)]
    pm = _swap_perm(MLA_ROPE)
    pg = _swap_perm(GQA_HEAD_DIM)
    pg8 = np.concatenate([pg + 64 * h for h in range(GQA_HEADS)])
    pg2 = np.concatenate([pg + 64 * h for h in range(GQA_KV_HEADS)])

    def pad_kpe(w):
        return jnp.pad(w, ((0, 0), (64, 32)))

    def dup(w):
        return jnp.concatenate([w[:, :64], w[:, :64], w[:, 64:], w[:, 64:]], axis=1)

    w_ext = jnp.concatenate(
        [qc, kvc, pad_kpe(kpe), pad_kpe(kpe[:, pm]), qg, qg[:, pg8],
         dup(kg), dup(kg[:, pg2]), gm, gg], axis=1).astype(BF16)

    wqb = mla_w_qb[l].reshape(MLA_Q_LORA, MLA_HEADS, MLA_NOPE + MLA_ROPE)
    rope_sw = wqb[:, :, MLA_NOPE:][:, :, pm]
    wqb_pad = jnp.pad(wqb, ((0, 0), (0, 0), (0, 32))).reshape(MLA_Q_LORA, 1024)
    wqbs_pad = jnp.pad(rope_sw, ((0, 0), (0, 0), (64, 32))).reshape(MLA_Q_LORA, 1024)

    wkvb = mla_w_kvb[l].reshape(MLA_KV_LORA, MLA_HEADS, MLA_NOPE + MLA_V)
    wkk = jnp.pad(wkvb[:, :, :MLA_NOPE], ((0, 0), (0, 0), (0, 64))).reshape(MLA_KV_LORA, 1024)
    wkv = wkvb[:, :, MLA_NOPE:].reshape(MLA_KV_LORA, MLA_HEADS * MLA_V)

    def pad_rows(w, heads):
        w = jnp.pad(w.reshape(heads, 64, d), ((0, 0), (0, 64), (0, 0)))
        return w.reshape(heads * LANE, d)

    wom = pad_rows(w_o_mla[l], MLA_HEADS)
    wog = pad_rows(w_o_gqa[l], GQA_HEADS)

    eye = np.kron(np.eye(8, dtype=np.float32), np.full((64, 64), 1.0 / 64, np.float32))
    return {
        "w_in": w_ext,
        "wqb": wqb_pad.astype(BF16),
        "wqbs": wqbs_pad.astype(BF16),
        "wkk": wkk.astype(BF16),
        "wkvt": wkv.T.astype(BF16),
        "wvgt": vg.T.astype(BF16),
        "bd": jnp.asarray(eye, BF16),
        "wom": wom.astype(BF16),
        "wog": wog.astype(BF16),
        "wout": w_out[l].astype(BF16),
    }


def kernel(x, c, ctx, c_ctx, w_ada, b_ada, norm1_g, norm2_g, w_in, mla_q_norm_g, mla_w_qb,
           mla_kv_norm_g, mla_w_kvb, gqa_q_norm_g, gqa_k_norm_g, w_o_mla, w_o_gqa, w_out,
           dense_w1, dense_w3, dense_w2, moe_router, moe_w1, moe_w3, moe_w2, final_norm_g):
    b, n_lat, d = x.shape
    n_ctx = ctx.shape[1]
    depth = w_ada.shape[0]
    t = n_ctx + n_lat
    tm, tq = TOKEN_TILE, QUERY_TILE
    assert n_ctx % tm == 0 and n_lat % tm == 0 and n_lat % GRID_W == 0
    assert n_lat % (2 * tq) == 0 and n_lat % n_ctx == 0 and n_ctx % KEY_BLOCK == 0

    rows = -(-(b + 1) // 8) * 8
    cc = jnp.zeros((rows, d), F32).at[:b].set(c).at[b].set(c_ctx)
    mod_all = _ada_call(cc, w_ada, b_ada).reshape(depth, rows, 6, d)

    cos_m, sin_m = _rope_parts(n_ctx, n_lat, MLA_ROPE)
    cos_g, sin_g = _rope_parts(n_ctx, n_lat, GQA_HEAD_DIM)

    xs = jnp.concatenate([x, ctx], axis=1)
    for l in range(depth):
        lw = _layer_weights(l, w_in, mla_w_qb, mla_w_kvb, w_o_mla, w_o_gqa, w_out)
        lw["gq"] = mla_q_norm_g[l][None, :]
        lw["gkv"] = mla_kv_norm_g[l][None, :]
        lw["g2"] = norm2_g[l][None, :]
        tabs = _layer_tables(cos_m, sin_m, cos_g, sin_g, gqa_q_norm_g[l], gqa_k_norm_g[l])
        mod = mod_all[l]
        moe = l % 2 == 1

        qm, km, vm, qg, kg, vg, gates = _inproj_call(
            xs, mod, norm1_g[l][None, :], lw, tabs, n_ctx=n_ctx, tm=tm)
        om = _attn_call(qm, km, vm, gqa=False, n_ctx=n_ctx, tq=tq)
        og = _attn_call(qg, kg, vg, gqa=True, n_ctx=n_ctx, tq=tq)

        if moe:
            r = jnp.pad(moe_router[l // 2], ((0, 0), (0, LANE - N_EXPERTS)))
            lw["r_hi"] = r.astype(BF16)
            lw["r_lo"] = (r - lw["r_hi"].astype(F32)).astype(BF16)
            xs, h2, comb = _post_call(om, og, gates, xs, mod, lw, n_ctx=n_ctx, tm=tm, moe=True)
            lw["w1"] = moe_w1[l // 2].astype(BF16)
            lw["w3"] = moe_w3[l // 2].astype(BF16)
            lw["w2"] = moe_w2[l // 2].astype(BF16)
            f = _moe_ffn_call(h2.reshape(b * t, d), comb.reshape(b * t, LANE), lw)
            f = f.reshape(b, t, d)
            if l == depth - 1:
                return _final_norm_call(xs, final_norm_g[None, :], n_ctx=n_ctx, tm=tm,
                                        f=f, mod=mod)
            xs = _residual_call(xs, f, mod, n_ctx=n_ctx, tm=tm)
        else:
            xs, h2 = _post_call(om, og, gates, xs, mod, lw, n_ctx=n_ctx, tm=tm, moe=False)
            lw["w1"] = dense_w1[l // 2].astype(BF16)
            lw["w3"] = dense_w3[l // 2].astype(BF16)
            lw["w2"] = dense_w2[l // 2].astype(BF16)
            xs = _dense_ffn_call(h2, xs, mod, lw, n_ctx=n_ctx, tm=tm)

    return _final_norm_call(xs, final_norm_g[None, :], n_ctx=n_ctx, tm=tm)
```

```python
import functools
import math

import numpy as np
import jax
import jax.numpy as jnp
from jax import lax
from jax.experimental import pallas as pl
from jax.experimental.pallas import tpu as pltpu

F32 = jnp.float32
BF16 = jnp.bfloat16

GRID_W = 64
ROPE_THETA = 10000.0
NORM_EPS = 1e-6
MLA_HEADS = 8
MLA_Q_LORA = 384
MLA_KV_LORA = 256
MLA_NOPE = 64
MLA_ROPE = 32
MLA_V = 64
MLA_SCALE = (MLA_NOPE + MLA_ROPE) ** -0.5
GQA_HEADS = 8
GQA_KV_HEADS = 2
GQA_GROUP = GQA_HEADS // GQA_KV_HEADS
GQA_HEAD_DIM = 64
GQA_SCALE = GQA_HEAD_DIM ** -0.5
N_EXPERTS = 8
LOG2E = math.log2(math.e)

LANE = 128
VMEM_LIMIT = 56 * 1024 * 1024
TOKEN_TILE = 256
QUERY_TILE = 256

C_QC = 0
C_KVC = C_QC + MLA_Q_LORA
C_KPE = C_KVC + MLA_KV_LORA
C_KPES = C_KPE + LANE
C_QG = C_KPES + LANE
C_QGS = C_QG + 512
C_KG = C_QGS + 512
C_KGS = C_KG + 256
C_GATE = C_KGS + 256
C_END = C_GATE + 2048

T_AM, T_BM, T_CK, T_SK, T_AQ, T_BQ, T_AK, T_BK = range(8)


def _cparams(sem):
    return pltpu.CompilerParams(dimension_semantics=sem, vmem_limit_bytes=VMEM_LIMIT)


def _const_spec(shape):
    nd = len(shape)
    return pl.BlockSpec(shape, lambda *_: (0,) * nd)


def _dot(a, b):
    return jnp.dot(a, b, preferred_element_type=F32)


def _rms(x):
    return x * lax.rsqrt(jnp.mean(x * x, axis=-1, keepdims=True) + NORM_EPS)


def _ada_kernel(cc_ref, w_ref, b_ref, o_ref):
    a = cc_ref[...]
    a = a * jax.nn.sigmoid(a)
    a_hi = a.astype(BF16)
    a_lo = (a - a_hi.astype(F32)).astype(BF16)
    w = w_ref[0]
    w_hi = w.astype(BF16)
    w_lo = (w - w_hi.astype(F32)).astype(BF16)
    acc = _dot(a_hi, w_hi) + _dot(a_lo, w_hi) + _dot(a_hi, w_lo)
    o_ref[0] = acc + b_ref[0]


def _ada_call(cc, w_ada, b_ada):
    depth, d, n = w_ada.shape
    rows = cc.shape[0]
    tn = 1536
    return pl.pallas_call(
        _ada_kernel,
        grid=(depth, n // tn),
        in_specs=[
            pl.BlockSpec((rows, d), lambda l, j: (0, 0)),
            pl.BlockSpec((1, d, tn), lambda l, j: (l, 0, j)),
            pl.BlockSpec((1, 1, tn), lambda l, j: (l, 0, j)),
        ],
        out_specs=pl.BlockSpec((1, rows, tn), lambda l, j: (l, 0, j)),
        out_shape=jax.ShapeDtypeStruct((depth, rows, n), F32),
        compiler_params=_cparams(("arbitrary", "arbitrary")),
        name="ada",
    )(cc, w_ada, b_ada.reshape(depth, 1, n))


def _group_mean_sq(x, bd):
    x2 = x * x
    hi = x2.astype(BF16)
    lo = (x2 - hi.astype(F32)).astype(BF16)
    return _dot(hi, bd) + _dot(lo, bd)


def _tile_lanes(t, n):
    return jnp.concatenate([t] * n, axis=1)


def _dot_nt(a, b):
    return lax.dot_general(a, b, (((1,), (1,)), ((), ())), preferred_element_type=F32)


def _inproj_kernel(x_ref, mod_ref, g1_ref, win_ref, gq_ref, wqb_ref, wqbs_ref, gkv_ref,
                   wkk_ref, wkvt_ref, wvgt_ref, bd_ref, tab_ref,
                   qm_ref, km_ref, vm_ref, qg_ref, kg_ref, vg_ref, gate_ref):
    x = x_ref[0]
    mod = mod_ref[0]
    h = (_rms(x) * g1_ref[...]) * (1.0 + mod[1:2]) + mod[0:1]
    hb = h.astype(BF16)

    y = _dot(hb, win_ref[...])

    def proj(lo, hi):
        return y[:, lo:hi]

    qcn = (_rms(proj(C_QC, C_KVC)) * gq_ref[...]).astype(BF16)
    q = _dot(qcn, wqb_ref[...])
    qs = _dot(qcn, wqbs_ref[...])
    am = _tile_lanes(tab_ref[T_AM], MLA_HEADS)
    bm = _tile_lanes(tab_ref[T_BM], MLA_HEADS)
    qm_ref[0] = (q * am + qs * bm).astype(BF16)

    kvn = (_rms(proj(C_KVC, C_KPE)) * gkv_ref[...]).astype(BF16)
    kpe = proj(C_KPE, C_KPES) * tab_ref[T_CK] + proj(C_KPES, C_QG) * tab_ref[T_SK]
    kn = _dot(kvn, wkk_ref[...])
    km_ref[0] = (kn + _tile_lanes(kpe, MLA_HEADS)).astype(BF16)
    tm = x.shape[0]
    vm_ref[0] = _dot_nt(wkvt_ref[...], kvn).astype(BF16).reshape(MLA_HEADS, MLA_V, tm)

    bd = bd_ref[...]
    qg = proj(C_QG, C_QGS)
    qgs = proj(C_QGS, C_KG)
    rq = lax.rsqrt(_group_mean_sq(qg, bd) + NORM_EPS)
    aq = _tile_lanes(tab_ref[T_AQ], 4)
    bq = _tile_lanes(tab_ref[T_BQ], 4)
    qg_ref[0] = (rq * (qg * aq + qgs * bq)).astype(BF16)

    kg = proj(C_KG, C_KGS)
    kgs = proj(C_KGS, C_GATE)
    rk = lax.rsqrt(_group_mean_sq(kg, bd[:256, :256]) + NORM_EPS)
    ak = _tile_lanes(tab_ref[T_AK], 2)
    bk = _tile_lanes(tab_ref[T_BK], 2)
    kg_ref[0] = (rk * (kg * ak + kgs * bk)).astype(BF16)
    vg_ref[0] = _dot_nt(wvgt_ref[...], hb).astype(BF16).reshape(
        GQA_KV_HEADS, GQA_HEAD_DIM, tm)

    gate_ref[0] = jax.nn.sigmoid(proj(C_GATE, C_END)).astype(BF16)


def _inproj_call(xs, mod, g1, lw, tabs, *, n_ctx, tm):
    b, t, d = xs.shape
    nlt = (t - n_ctx) // tm
    ctx_row = b

    def tok(i, bb):
        return (bb, i, 0)

    def modmap(i, bb):
        return (jnp.where(i >= nlt, ctx_row, bb), 0, 0)

    def tok_spec(w):
        return pl.BlockSpec((1, tm, w), tok), jax.ShapeDtypeStruct((b, t, w), BF16)

    def vt_spec(heads):
        return (pl.BlockSpec((1, heads, 64, tm), lambda i, bb: (bb, 0, 0, i)),
                jax.ShapeDtypeStruct((b, heads, 64, t), BF16))

    outs = [tok_spec(1024), tok_spec(1024), vt_spec(MLA_HEADS),
            tok_spec(512), tok_spec(256), vt_spec(GQA_KV_HEADS), tok_spec(2048)]
    return pl.pallas_call(
        _inproj_kernel,
        grid=(t // tm, b),
        in_specs=[
            pl.BlockSpec((1, tm, d), tok),
            pl.BlockSpec((1, 6, d), modmap),
            _const_spec((1, d)),
            _const_spec((d, C_END)),
            _const_spec((1, MLA_Q_LORA)),
            _const_spec((MLA_Q_LORA, 1024)),
            _const_spec((MLA_Q_LORA, 1024)),
            _const_spec((1, MLA_KV_LORA)),
            _const_spec((MLA_KV_LORA, 1024)),
            _const_spec((MLA_HEADS * MLA_V, MLA_KV_LORA)),
            _const_spec((GQA_KV_HEADS * GQA_HEAD_DIM, d)),
            _const_spec((512, 512)),
            pl.BlockSpec((8, tm, LANE), lambda i, bb: (0, i, 0)),
        ],
        out_specs=[o[0] for o in outs],
        out_shape=[o[1] for o in outs],
        compiler_params=_cparams(("parallel", "arbitrary")),
        name="inproj",
    )(xs, mod, g1, lw["w_in"], lw["gq"], lw["wqb"], lw["wqbs"], lw["gkv"],
      lw["wkk"], lw["wkvt"], lw["wvgt"], lw["bd"], tabs)


KEY_BLOCK = 256


def _key_block(nkeys):
    assert nkeys % KEY_BLOCK == 0, nkeys
    return KEY_BLOCK


def _mask_gqa_q(q):
    lane = lax.broadcasted_iota(jnp.int32, q.shape, 1)
    keep = (lane // 64) == (pl.program_id(1) % 2)
    return jnp.where(keep, q, jnp.zeros_like(q))


def _attn_scores(q, k_ref, s_ref, m_ref, nkeys, probs=None, zero_ref=None):
    kb = _key_block(nkeys)
    mrun = lrun = None
    if probs is not None:
        s2_ref, p2_ref, m2_ref, l2_ref = probs
        m2 = m2_ref[0:1, :]
        zero = zero_ref[0:1, :]
    for lo in range(0, nkeys, kb):
        k = k_ref[0, lo:lo + kb, :]
        if probs is not None:
            p = jnp.exp2(s2_ref[lo:lo + kb, :] - m2)
            p2_ref[lo:lo + kb, :] = p.astype(BF16)
            f = jnp.sum(p, axis=0, keepdims=True)
            lrun = f if lrun is None else lrun + f
            tie = pltpu.bitcast(f, jnp.uint32) & zero
            tie = tie[:, :LANE] | tie[:, LANE:]
            k = pltpu.bitcast(pltpu.bitcast(k, jnp.uint32) | tie, BF16)
        s = lax.dot_general(k, q, (((1,), (1,)), ((), ())), preferred_element_type=F32)
        s_ref[lo:lo + kb, :] = s
        f = jnp.max(s, axis=0, keepdims=True)
        mrun = f if mrun is None else jnp.maximum(mrun, f)
    m_ref[...] = jnp.broadcast_to(mrun, m_ref.shape)
    if probs is not None:
        l2_ref[...] = jnp.broadcast_to(lrun, l2_ref.shape)


def _attn_probs(s_ref, p_ref, m_ref, l_ref, nkeys):
    kb = _key_block(nkeys)
    m = m_ref[0:1, :]
    lrun = None
    for lo in range(0, nkeys, kb):
        p = jnp.exp2(s_ref[lo:lo + kb, :] - m)
        f = jnp.sum(p, axis=0, keepdims=True)
        lrun = f if lrun is None else lrun + f
        p_ref[lo:lo + kb, :] = p.astype(BF16)
    l_ref[...] = jnp.broadcast_to(lrun, l_ref.shape)


def _attn_values(p_ref, vt_ref, l_ref, nkeys):
    kb = _key_block(nkeys)
    acc = None
    for lo in range(0, nkeys, kb):
        part = _dot(vt_ref[0, 0, :, lo:lo + kb], p_ref[lo:lo + kb, :])
        acc = part if acc is None else acc + part
    o_t = acc / l_ref[0:1, :]
    o_t = jnp.concatenate([o_t, jnp.zeros_like(o_t)], axis=0)
    return o_t.T.astype(BF16)


def _attn_ctx_kernel(q_ref, k_ref, vt_ref, o_ref, s_scr, p_scr, m_scr, l_scr, *, gqa):
    q = q_ref[0]
    if gqa:
        q = _mask_gqa_q(q)
    nkeys = k_ref.shape[1]
    _attn_scores(q, k_ref, s_scr, m_scr, nkeys)
    _attn_probs(s_scr, p_scr, m_scr, l_scr, nkeys)
    o_ref[0] = _attn_values(p_scr, vt_ref, l_scr, nkeys)


def _attn_lat_kernel(q_ref, k_ref, vt_ref, zero_ref, oin_ref, o_ref,
                     s0, s1, p0, p1, m0, m1, l0, l1, *, gqa, tq, nkeys):
    del oin_ref
    j = pl.program_id(2)
    last = pl.num_programs(2) - 1
    s_scr, p_scr, m_scr, l_scr = (s0, s1), (p0, p1), (m0, m1), (l0, l1)

    def scores(slot, tied_probs=False):
        q = q_ref[0, slot * tq:(slot + 1) * tq, :]
        if gqa:
            q = _mask_gqa_q(q)
        other = None
        if tied_probs:
            o = 1 - slot
            other = (s_scr[o], p_scr[o], m_scr[o], l_scr[o])
        _attn_scores(q, k_ref, s_scr[slot], m_scr[slot], nkeys, other, zero_ref)

    def probs(slot):
        _attn_probs(s_scr[slot], p_scr[slot], m_scr[slot], l_scr[slot], nkeys)

    def values(slot):
        o_ref[0, slot * tq:(slot + 1) * tq, :] = _attn_values(
            p_scr[slot], vt_ref, l_scr[slot], nkeys)

    @pl.when(j == 0)
    def _():
        scores(0)
        scores(1)
        probs(0)

    @pl.when((j > 0) & (j < last))
    def _():
        scores(0, tied_probs=True)
        values(0)
        scores(1)
        probs(0)
        values(1)

    @pl.when(j == last)
    def _():
        probs(1)
        values(0)
        values(1)


def _attn_call(q, k, vt, *, gqa, n_ctx, tq):
    b, t, _ = q.shape
    n_lat = t - n_ctx
    heads = GQA_HEADS if gqa else MLA_HEADS
    if gqa:
        qcol = lambda h: h // 2
        kcol = lambda h: h // GQA_GROUP
        vhead = kcol
    else:
        qcol = lambda h: h
        kcol = lambda h: h
        vhead = lambda h: h
    out_shape = jax.ShapeDtypeStruct((b, t, heads * LANE), BF16)
    sem = ("parallel", "parallel", "arbitrary")

    cblk = n_lat // n_ctx
    o_ctx = pl.pallas_call(
        functools.partial(_attn_ctx_kernel, gqa=gqa),
        grid=(b, heads),
        in_specs=[
            pl.BlockSpec((1, n_ctx, LANE), lambda bb, h: (bb, cblk, qcol(h))),
            pl.BlockSpec((1, n_ctx, LANE), lambda bb, h: (bb, cblk, kcol(h))),
            pl.BlockSpec((1, 1, 64, n_ctx), lambda bb, h: (bb, vhead(h), 0, cblk)),
        ],
        out_specs=pl.BlockSpec((1, n_ctx, LANE), lambda bb, h: (bb, cblk, h)),
        out_shape=out_shape,
        scratch_shapes=[pltpu.VMEM((n_ctx, n_ctx), F32), pltpu.VMEM((n_ctx, n_ctx), BF16),
                        pltpu.VMEM((8, n_ctx), F32), pltpu.VMEM((8, n_ctx), F32)],
        compiler_params=_cparams(sem[:2]),
        name="attn_ctx_gqa" if gqa else "attn_ctx_mla",
    )(q, k, vt)

    npairs = n_lat // (2 * tq)
    kern = functools.partial(_attn_lat_kernel, gqa=gqa, tq=tq, nkeys=t)
    return pl.pallas_call(
        kern,
        grid=(b, heads, npairs + 1),
        in_specs=[
            pl.BlockSpec((1, 2 * tq, LANE),
                         lambda bb, h, j: (bb, jnp.minimum(j, npairs - 1), qcol(h))),
            pl.BlockSpec((1, t, LANE), lambda bb, h, j: (bb, 0, kcol(h))),
            pl.BlockSpec((1, 1, 64, t), lambda bb, h, j: (bb, vhead(h), 0, 0)),
            pl.BlockSpec((8, tq), lambda bb, h, j: (0, 0)),
            pl.BlockSpec(memory_space=pl.ANY),
        ],
        out_specs=pl.BlockSpec((1, 2 * tq, LANE),
                               lambda bb, h, j: (bb, jnp.maximum(j - 1, 0), h)),
        out_shape=out_shape,
        scratch_shapes=[pltpu.VMEM((t, tq), F32)] * 2 + [pltpu.VMEM((t, tq), BF16)] * 2
        + [pltpu.VMEM((8, tq), F32)] * 4,
        input_output_aliases={4: 0},
        compiler_params=_cparams(sem),
        name="attn_gqa" if gqa else "attn_mla",
    )(q, k, vt, jnp.zeros((8, tq), jnp.uint32), o_ctx)


def _top2_combine(logits):
    lane = lax.broadcasted_iota(jnp.int32, logits.shape, 1)
    neg = jnp.float32(-jnp.inf)
    lg = jnp.where(lane < N_EXPERTS, logits, neg)
    m1 = jnp.max(lg, axis=-1, keepdims=True)
    i1 = jnp.min(jnp.where(lg == m1, lane, LANE), axis=-1, keepdims=True)
    sel1 = lane == i1
    lg2 = jnp.where(sel1, neg, lg)
    m2 = jnp.max(lg2, axis=-1, keepdims=True)
    i2 = jnp.min(jnp.where(lg2 == m2, lane, LANE), axis=-1, keepdims=True)
    sel2 = lane == i2
    e2 = jnp.exp(m2 - m1)
    w1 = 1.0 / (1.0 + e2)
    w2 = e2 / (1.0 + e2)
    return jnp.where(sel1, w1, 0.0) + jnp.where(sel2, w2, 0.0)


def _post_kernel(*refs, moe):
    if moe:
        (om_ref, og_ref, gate_ref, x_ref, mod_ref, wom_ref, wog_ref, wout_ref, g2_ref,
         rhi_ref, rlo_ref, xo_ref, h2_ref, comb_ref) = refs
    else:
        (om_ref, og_ref, gate_ref, x_ref, mod_ref, wom_ref, wog_ref, wout_ref, g2_ref,
         xo_ref, h2_ref) = refs
    a = _dot(om_ref[0], wom_ref[...])
    bb = _dot(og_ref[0], wog_ref[...])
    gate = gate_ref[0]
    merged = gate[:, :1024].astype(F32) * a + gate[:, 1024:].astype(F32) * bb
    mix = _dot(merged.astype(BF16), wout_ref[...])
    mod = mod_ref[0]
    x1 = x_ref[0] + mod[2:3] * mix
    xo_ref[0] = x1
    h2 = (_rms(x1) * g2_ref[...]) * (1.0 + mod[4:5]) + mod[3:4]
    h2_ref[0] = h2.astype(BF16)
    if moe:
        h_hi = h2.astype(BF16)
        h_lo = (h2 - h_hi.astype(F32)).astype(BF16)
        logits = (_dot(h_hi, rhi_ref[...]) + _dot(h_lo, rhi_ref[...])
                  + _dot(h_hi, rlo_ref[...]))
        comb_ref[0] = _top2_combine(logits)


def _post_call(om, og, gates, xs, mod, lw, *, n_ctx, tm, moe):
    b, t, d = xs.shape
    nlt = (t - n_ctx) // tm
    ctx_row = b

    def tok(i, bb):
        return (bb, i, 0)

    def modmap(i, bb):
        return (jnp.where(i >= nlt, ctx_row, bb), 0, 0)

    in_specs = [
        pl.BlockSpec((1, tm, MLA_HEADS * LANE), tok),
        pl.BlockSpec((1, tm, GQA_HEADS * LANE), tok),
        pl.BlockSpec((1, tm, 2048), tok),
        pl.BlockSpec((1, tm, d), tok),
        pl.BlockSpec((1, 6, d), modmap),
        _const_spec((MLA_HEADS * LANE, d)),
        _const_spec((GQA_HEADS * LANE, d)),
        _const_spec((d, d)),
        _const_spec((1, d)),
    ]
    args = [om, og, gates, xs, mod, lw["wom"], lw["wog"], lw["wout"], lw["g2"]]
    out_specs = [pl.BlockSpec((1, tm, d), tok), pl.BlockSpec((1, tm, d), tok)]
    out_shape = [jax.ShapeDtypeStruct((b, t, d), F32), jax.ShapeDtypeStruct((b, t, d), BF16)]
    if moe:
        in_specs += [_const_spec((d, LANE)), _const_spec((d, LANE))]
        args += [lw["r_hi"], lw["r_lo"]]
        out_specs.append(pl.BlockSpec((1, tm, LANE), tok))
        out_shape.append(jax.ShapeDtypeStruct((b, t, LANE), F32))
    return pl.pallas_call(
        functools.partial(_post_kernel, moe=moe),
        grid=(t // tm, b),
        in_specs=in_specs,
        out_specs=out_specs,
        out_shape=out_shape,
        input_output_aliases={3: 0},
        compiler_params=_cparams(("parallel", "arbitrary")),
        name="post_moe" if moe else "post_dense",
    )(*args)


def _swiglu_chunk(hb, w1, w3, w2):
    a = _dot(hb, w1)
    g = (a * jax.nn.sigmoid(a)) * _dot(hb, w3)
    return _dot(g.astype(BF16), w2)


def _dense_ffn_kernel(h_ref, x_ref, mod_ref, w13_ref, w2_ref, o_ref):
    f = w2_ref.shape[0]
    ab = _dot(h_ref[0], w13_ref[...])
    a = ab[:, :f]
    g = (a * jax.nn.sigmoid(a)) * ab[:, f:]
    o_ref[0] = x_ref[0] + mod_ref[0][5:6] * _dot(g.astype(BF16), w2_ref[...])


def _dense_ffn_call(h2, xs, mod, lw, *, n_ctx, tm):
    b, t, d = xs.shape
    f = lw["w2"].shape[0]
    nlt = (t - n_ctx) // tm
    ctx_row = b

    def tok(i, bb):
        return (bb, i, 0)

    def modmap(i, bb):
        return (jnp.where(i >= nlt, ctx_row, bb), 0, 0)

    return pl.pallas_call(
        _dense_ffn_kernel,
        grid=(t // tm, b),
        in_specs=[
            pl.BlockSpec((1, tm, d), tok),
            pl.BlockSpec((1, tm, d), tok),
            pl.BlockSpec((1, 6, d), modmap),
            pl.BlockSpec((d, 2 * f), lambda i, bb: (0, 0), pipeline_mode=pl.Buffered(1)),
            pl.BlockSpec((f, d), lambda i, bb: (0, 0), pipeline_mode=pl.Buffered(1)),
        ],
        out_specs=pl.BlockSpec((1, tm, d), tok),
        out_shape=jax.ShapeDtypeStruct((b, t, d), F32),
        input_output_aliases={1: 0},
        compiler_params=_cparams(("parallel", "arbitrary")),
        name="ffn_dense",
    )(h2, xs, mod, lw["w13"], lw["w2"])


MOE_F_SPLIT = 2


def _moe_tile(n):
    for tb in (1024, 768, 512, 256):
        if n % tb == 0:
            return tb
    raise ValueError(n)


def _moe_rows(tb):
    half = tb // N_EXPERTS
    return -(-(half + half // 8) // 16) * 16


def _moe_ffn_kernel(h_ref, comb_ref, tri_ref, w1_ref, w3_ref, w2_ref, o_ref,
                    sel_scr, pos_scr, xg_scr, y_scr, *, rows):
    e = pl.program_id(1)
    fh = pl.program_id(2)
    tb = h_ref.shape[0]

    @pl.when((e == 0) & (fh == 0))
    def _():
        sel_t = (comb_ref[...].T[:16] > 0.0).astype(BF16)
        sel_scr[...] = sel_t.astype(F32)
        pos_scr[...] = lax.dot_general(sel_t, tri_ref[...], (((1,), (1,)), ((), ())),
                                       preferred_element_type=F32)
        o_ref[...] = jnp.zeros_like(o_ref)

    sel_e = sel_scr[pl.ds(e, 1), :]
    rank_e = jnp.where(sel_e > 0.0, pos_scr[pl.ds(e, 1), :], -1.0)
    count = jnp.sum(sel_e).astype(jnp.int32)
    nsub = (count + rows - 1) // rows

    def one_hot(s):
        j = lax.broadcasted_iota(jnp.int32, (rows, tb), 0) + s * rows
        return jnp.where(rank_e == j.astype(F32), 1.0, 0.0).astype(BF16)

    def block(s):
        return pl.ds(pl.multiple_of(s * rows, 16), rows)

    @pl.when(fh == 0)
    def _():
        def gather(s, carry):
            xg_scr[block(s), :] = _dot(one_hot(s), h_ref[...]).astype(BF16)
            return carry
        lax.fori_loop(0, nsub, gather, 0)

    def expert(s, carry):
        part = _swiglu_chunk(xg_scr[block(s), :], w1_ref[0], w3_ref[0], w2_ref[0])

        @pl.when(fh == 0)
        def _():
            y_scr[block(s), :] = part

        @pl.when(fh != 0)
        def _():
            y_scr[block(s), :] = y_scr[block(s), :] + part
        return carry
    lax.fori_loop(0, nsub, expert, 0)

    @pl.when(fh == pl.num_programs(2) - 1)
    def _():
        comb = comb_ref[...]
        lane = lax.broadcasted_iota(jnp.int32, comb.shape, 1)
        w_col = jnp.sum(jnp.where(lane == e, comb, 0.0), axis=-1, keepdims=True)

        def scatter(s, carry):
            z = lax.dot_general(one_hot(s), y_scr[block(s), :].astype(BF16),
                                (((0,), (0,)), ((), ())), preferred_element_type=F32)
            o_ref[...] += w_col * z
            return carry
        lax.fori_loop(0, nsub, scatter, 0)


def _moe_ffn_call(h2, comb, lw):
    n, d = h2.shape
    ne, _, f = lw["w1"].shape
    fc = f // MOE_F_SPLIT
    tb = _moe_tile(n)
    rows = _moe_rows(tb)
    nsub_max = -(-tb // rows)
    tri = (lax.broadcasted_iota(jnp.int32, (tb, tb), 1)
           < lax.broadcasted_iota(jnp.int32, (tb, tb), 0)).astype(BF16)
    return pl.pallas_call(
        functools.partial(_moe_ffn_kernel, rows=rows),
        grid=(n // tb, ne, MOE_F_SPLIT),
        in_specs=[
            pl.BlockSpec((tb, d), lambda i, e, fh: (i, 0)),
            pl.BlockSpec((tb, LANE), lambda i, e, fh: (i, 0)),
            pl.BlockSpec((tb, tb), lambda i, e, fh: (0, 0), pipeline_mode=pl.Buffered(1)),
            pl.BlockSpec((1, d, fc), lambda i, e, fh: (e, 0, fh)),
            pl.BlockSpec((1, d, fc), lambda i, e, fh: (e, 0, fh)),
            pl.BlockSpec((1, fc, d), lambda i, e, fh: (e, fh, 0)),
        ],
        out_specs=pl.BlockSpec((tb, d), lambda i, e, fh: (i, 0)),
        out_shape=jax.ShapeDtypeStruct((n, d), F32),
        scratch_shapes=[pltpu.VMEM((16, tb), F32), pltpu.VMEM((16, tb), F32),
                        pltpu.VMEM((nsub_max * rows, d), BF16),
                        pltpu.VMEM((nsub_max * rows, d), F32)],
        compiler_params=_cparams(("parallel", "arbitrary", "arbitrary")),
        name="ffn_moe",
    )(h2, comb, tri, lw["w1"], lw["w3"], lw["w2"])


def _residual_kernel(x_ref, f_ref, mod_ref, o_ref):
    o_ref[0] = x_ref[0] + mod_ref[0][5:6] * f_ref[0]


def _residual_call(xs, f, mod, *, n_ctx, tm):
    b, t, d = xs.shape
    nlt = (t - n_ctx) // tm
    ctx_row = b

    def tok(i, bb):
        return (bb, i, 0)

    def modmap(i, bb):
        return (jnp.where(i >= nlt, ctx_row, bb), 0, 0)

    return pl.pallas_call(
        _residual_kernel,
        grid=(t // tm, b),
        in_specs=[pl.BlockSpec((1, tm, d), tok), pl.BlockSpec((1, tm, d), tok),
                  pl.BlockSpec((1, 6, d), modmap)],
        out_specs=pl.BlockSpec((1, tm, d), tok),
        out_shape=jax.ShapeDtypeStruct((b, t, d), F32),
        input_output_aliases={0: 0},
        compiler_params=_cparams(("parallel", "arbitrary")),
        name="moe_residual",
    )(xs, f, mod)


def _final_norm_kernel(x_ref, g_ref, o_ref):
    o_ref[0] = _rms(x_ref[0]) * g_ref[...]


def _final_norm_res_kernel(x_ref, g_ref, f_ref, mod_ref, o_ref):
    o_ref[0] = _rms(x_ref[0] + mod_ref[0][5:6] * f_ref[0]) * g_ref[...]


def _final_norm_call(xs, g, *, n_ctx, tm, f=None, mod=None):
    b, t, d = xs.shape
    n_lat = t - n_ctx
    tok = pl.BlockSpec((1, tm, d), lambda i, bb: (bb, i, 0))
    in_specs = [tok, _const_spec((1, d))]
    args = [xs, g]
    kern = _final_norm_kernel
    if f is not None:
        in_specs += [tok, pl.BlockSpec((1, 6, d), lambda i, bb: (bb, 0, 0))]
        args += [f, mod]
        kern = _final_norm_res_kernel
    return pl.pallas_call(
        kern,
        grid=(n_lat // tm, b),
        in_specs=in_specs,
        out_specs=tok,
        out_shape=jax.ShapeDtypeStruct((b, n_lat, d), F32),
        compiler_params=_cparams(("parallel", "arbitrary")),
        name="final_norm",
    )(*args)


def _rope_parts(n_ctx, n_lat, dim):
    half = dim // 2
    quarter = half // 2
    inv = ROPE_THETA ** (-(jnp.arange(quarter, dtype=F32) * 2.0) / half)
    tpos = jnp.arange(n_lat)
    row = (tpos // GRID_W).astype(F32)
    col = (tpos % GRID_W).astype(F32)
    zeros = jnp.zeros((n_ctx, quarter), F32)
    ang_r = jnp.concatenate([row[:, None] * inv, zeros], axis=0)
    ang_c = jnp.concatenate([col[:, None] * inv, zeros], axis=0)
    cr, sr, cc, sc = jnp.cos(ang_r), jnp.sin(ang_r), jnp.cos(ang_c), jnp.sin(ang_c)
    cos = jnp.concatenate([cr, cr, cc, cc], axis=1)
    sin = jnp.concatenate([-sr, sr, -sc, sc], axis=1)
    return cos, sin


def _swap_perm(dim):
    q = dim // 4
    idx = np.arange(dim)
    return np.concatenate([idx[q:2 * q], idx[:q], idx[3 * q:], idx[2 * q:3 * q]])


def _layer_tables(cos_m, sin_m, cos_g, sin_g, gq, gk):
    t = cos_m.shape[0]
    z64 = jnp.zeros((t, 64), F32)
    z32 = jnp.zeros((t, 32), F32)
    sm = MLA_SCALE * LOG2E
    sg = GQA_SCALE * LOG2E
    pg = _swap_perm(GQA_HEAD_DIM)
    am = jnp.concatenate([jnp.full((t, 64), sm, F32), sm * cos_m, z32], axis=1)
    bm = jnp.concatenate([z64, sm * sin_m, z32], axis=1)
    ck = jnp.concatenate([z64, cos_m, z32], axis=1)
    sk = jnp.concatenate([z64, sin_m, z32], axis=1)
    aq = sg * cos_g * gq[None, :]
    bq = sg * sin_g * gq[pg][None, :]
    ak = cos_g * gk[None, :]
    bk = sin_g * gk[pg][None, :]
    two = lambda a: jnp.concatenate([a, a], axis=1)
    return jnp.stack([am, bm, ck, sk, two(aq), two(bq), two(ak), two(bk)], axis=0)


def _layer_weights(l, w_in, mla_w_qb, mla_w_kvb, w_o_mla, w_o_gqa, w_out):
    d = w_in.shape[1]
    wl = w_in[l]
    offs = np.cumsum([0, MLA_Q_LORA, MLA_KV_LORA, MLA_ROPE, 512, 128, 128, d, d])
    qc, kvc, kpe, qg, kg, vg, gm, gg = [wl[:, offs[i]:offs[i + 1]] for i in range(8)]
    pm = _swap_perm(MLA_ROPE)
    pg = _swap_perm(GQA_HEAD_DIM)
    pg8 = np.concatenate([pg + 64 * h for h in range(GQA_HEADS)])
    pg2 = np.concatenate([pg + 64 * h for h in range(GQA_KV_HEADS)])

    def pad_kpe(w):
        return jnp.pad(w, ((0, 0), (64, 32)))

    def dup(w):
        return jnp.concatenate([w[:, :64], w[:, :64], w[:, 64:], w[:, 64:]], axis=1)

    w_ext = jnp.concatenate(
        [qc, kvc, pad_kpe(kpe), pad_kpe(kpe[:, pm]), qg, qg[:, pg8],
         dup(kg), dup(kg[:, pg2]), gm, gg], axis=1).astype(BF16)

    wqb = mla_w_qb[l].reshape(MLA_Q_LORA, MLA_HEADS, MLA_NOPE + MLA_ROPE)
    rope_sw = wqb[:, :, MLA_NOPE:][:, :, pm]
    wqb_pad = jnp.pad(wqb, ((0, 0), (0, 0), (0, 32))).reshape(MLA_Q_LORA, 1024)
    wqbs_pad = jnp.pad(rope_sw, ((0, 0), (0, 0), (64, 32))).reshape(MLA_Q_LORA, 1024)

    wkvb = mla_w_kvb[l].reshape(MLA_KV_LORA, MLA_HEADS, MLA_NOPE + MLA_V)
    wkk = jnp.pad(wkvb[:, :, :MLA_NOPE], ((0, 0), (0, 0), (0, 64))).reshape(MLA_KV_LORA, 1024)
    wkv = wkvb[:, :, MLA_NOPE:].reshape(MLA_KV_LORA, MLA_HEADS * MLA_V)

    def pad_rows(w, heads):
        w = jnp.pad(w.reshape(heads, 64, d), ((0, 0), (0, 64), (0, 0)))
        return w.reshape(heads * LANE, d)

    wom = pad_rows(w_o_mla[l], MLA_HEADS)
    wog = pad_rows(w_o_gqa[l], GQA_HEADS)

    eye = np.kron(np.eye(8, dtype=np.float32), np.full((64, 64), 1.0 / 64, np.float32))
    return {
        "w_in": w_ext,
        "wqb": wqb_pad.astype(BF16),
        "wqbs": wqbs_pad.astype(BF16),
        "wkk": wkk.astype(BF16),
        "wkvt": wkv.T.astype(BF16),
        "wvgt": vg.T.astype(BF16),
        "bd": jnp.asarray(eye, BF16),
        "wom": wom.astype(BF16),
        "wog": wog.astype(BF16),
        "wout": w_out[l].astype(BF16),
    }


def kernel(x, c, ctx, c_ctx, w_ada, b_ada, norm1_g, norm2_g, w_in, mla_q_norm_g, mla_w_qb,
           mla_kv_norm_g, mla_w_kvb, gqa_q_norm_g, gqa_k_norm_g, w_o_mla, w_o_gqa, w_out,
           dense_w1, dense_w3, dense_w2, moe_router, moe_w1, moe_w3, moe_w2, final_norm_g):
    b, n_lat, d = x.shape
    n_ctx = ctx.shape[1]
    depth = w_ada.shape[0]
    t = n_ctx + n_lat
    tm, tq = TOKEN_TILE, QUERY_TILE
    assert n_ctx % tm == 0 and n_lat % tm == 0 and n_lat % GRID_W == 0
    assert n_lat % (2 * tq) == 0 and n_lat % n_ctx == 0 and n_ctx % KEY_BLOCK == 0

    rows = -(-(b + 1) // 8) * 8
    cc = jnp.zeros((rows, d), F32).at[:b].set(c).at[b].set(c_ctx)
    mod_all = _ada_call(cc, w_ada, b_ada).reshape(depth, rows, 6, d)

    cos_m, sin_m = _rope_parts(n_ctx, n_lat, MLA_ROPE)
    cos_g, sin_g = _rope_parts(n_ctx, n_lat, GQA_HEAD_DIM)

    xs = jnp.concatenate([x, ctx], axis=1)
    for l in range(depth):
        lw = _layer_weights(l, w_in, mla_w_qb, mla_w_kvb, w_o_mla, w_o_gqa, w_out)
        lw["gq"] = mla_q_norm_g[l][None, :]
        lw["gkv"] = mla_kv_norm_g[l][None, :]
        lw["g2"] = norm2_g[l][None, :]
        tabs = _layer_tables(cos_m, sin_m, cos_g, sin_g, gqa_q_norm_g[l], gqa_k_norm_g[l])
        mod = mod_all[l]
        moe = l % 2 == 1

        qm, km, vm, qg, kg, vg, gates = _inproj_call(
            xs, mod, norm1_g[l][None, :], lw, tabs, n_ctx=n_ctx, tm=tm)
        om = _attn_call(qm, km, vm, gqa=False, n_ctx=n_ctx, tq=tq)
        og = _attn_call(qg, kg, vg, gqa=True, n_ctx=n_ctx, tq=tq)

        if moe:
            r = jnp.pad(moe_router[l // 2], ((0, 0), (0, LANE - N_EXPERTS)))
            lw["r_hi"] = r.astype(BF16)
            lw["r_lo"] = (r - lw["r_hi"].astype(F32)).astype(BF16)
            xs, h2, comb = _post_call(om, og, gates, xs, mod, lw, n_ctx=n_ctx, tm=tm, moe=True)
            lw["w1"] = moe_w1[l // 2].astype(BF16)
            lw["w3"] = moe_w3[l // 2].astype(BF16)
            lw["w2"] = moe_w2[l // 2].astype(BF16)
            f = _moe_ffn_call(h2.reshape(b * t, d), comb.reshape(b * t, LANE), lw)
            f = f.reshape(b, t, d)
            if l == depth - 1:
                return _final_norm_call(xs, final_norm_g[None, :], n_ctx=n_ctx, tm=tm,
                                        f=f, mod=mod)
            xs = _residual_call(xs, f, mod, n_ctx=n_ctx, tm=tm)
        else:
            xs, h2 = _post_call(om, og, gates, xs, mod, lw, n_ctx=n_ctx, tm=tm, moe=False)
            lw["w13"] = jnp.concatenate(
                [dense_w1[l // 2], dense_w3[l // 2]], axis=1).astype(BF16)
            lw["w2"] = dense_w2[l // 2].astype(BF16)
            xs = _dense_ffn_call(h2, xs, mod, lw, n_ctx=n_ctx, tm=tm)

    return _final_norm_call(xs, final_norm_g[None, :], n_ctx=n_ctx, tm=tm)
```
